```python
import jax, jax.numpy as jnp
from jax import lax
import numpy as np

D_MODEL = 2048
BATCH = 2
SEQ = 16384
DEPTH = 2

HEAD_DIM = 128
ROPE_THETA = 500000.0
ROT_DIM = HEAD_DIM // 4
NORM_EPS = 1e-6
DIL_GROUPS = ((128, 1), (512, 4), (2048, 16))
A_HEADS_PER_GROUP = 4
A_HEADS = A_HEADS_PER_GROUP * len(DIL_GROUPS)
A_BLOCK = 128
B_HEADS = 12
B_TOPK_MAX = 256
B_QBLOCK = 128
IDX_HEADS = 16
IDX_DIM = 64
IDX_ROT = IDX_DIM // 4
LRU_WIDTH = D_MODEL
LRU_BLOCKS = 16
LRU_BLOCK_W = LRU_WIDTH // LRU_BLOCKS
LRU_C = 8.0
LRU_CONV_W = 4
D_FF = 3 * D_MODEL
FFN_CONV_W = 3
N_BRANCH = 3
A_WIDTH = A_HEADS * HEAD_DIM
B_WIDTH = B_HEADS * HEAD_DIM
A_OUT_WIDTH = A_HEADS_PER_GROUP * HEAD_DIM
IN_SPLITS = (A_WIDTH, A_WIDTH, A_WIDTH, B_WIDTH, HEAD_DIM, HEAD_DIM,
             IDX_HEADS * IDX_DIM, IDX_DIM, IDX_HEADS, LRU_WIDTH, LRU_WIDTH, N_BRANCH * D_MODEL)
IN_COLS = sum(IN_SPLITS)

kernel_name = 'hybrid_gated_dilated_dsa_rglru_block'


def rms_norm(x, g):
    xf = x.astype(jnp.float32)
    y = xf * lax.rsqrt(jnp.mean(xf * xf, axis=-1, keepdims=True) + NORM_EPS)
    return (y * g.astype(jnp.float32)).astype(x.dtype)


def rope_partial(x, pos, rot_dim):
    half = rot_dim // 2
    freq = ROPE_THETA ** (-jnp.arange(half, dtype=jnp.float32) / half)
    ang = pos.astype(jnp.float32)[..., None] * freq
    cos = jnp.cos(ang)[:, :, None, :]
    sin = jnp.sin(ang)[:, :, None, :]
    xf = x.astype(jnp.float32)
    x1, x2, rest = xf[..., :half], xf[..., half:rot_dim], xf[..., rot_dim:]
    out = jnp.concatenate([x1 * cos - x2 * sin, x2 * cos + x1 * sin, rest], axis=-1)
    return out.astype(x.dtype)


def causal_dwconv(x, w, b):
    width = w.shape[0]
    s = x.shape[1]
    xp = jnp.pad(x, ((0, 0), (width - 1, 0), (0, 0)))
    y = b.astype(x.dtype)
    for tap in range(width):
        y = y + xp[:, tap:tap + s] * w[tap].astype(x.dtype)
    return y


def split_columns(t, sizes):
    parts, start = [], 0
    for size in sizes:
        parts.append(t[..., start:start + size])
        start += size
    return parts


def dilated_group_attention(q, k, v, window, dil):
    bsz, s, h, hd = q.shape
    n = s // dil
    span = window // dil
    nb = -(-n // A_BLOCK)
    pad = nb * A_BLOCK - n

    def to_strided(t):
        t = t.reshape(bsz, n, dil, h, hd).transpose(0, 2, 1, 3, 4)
        t = jnp.pad(t, ((0, 0), (0, 0), (0, pad), (0, 0), (0, 0)))
        return t.reshape(bsz, dil, nb, A_BLOCK, h, hd)

    def with_prev(t):
        prev = jnp.pad(t[:, :, :-1], ((0, 0), (0, 0), (1, 0), (0, 0), (0, 0), (0, 0)))
        return jnp.concatenate([prev, t], axis=3)

    qs, ks, vs = to_strided(q), to_strided(k), to_strided(v)
    kk, vv = with_prev(ks), with_prev(vs)
    scores = jnp.einsum('brnqhd,brnkhd->brnhqk', qs, kk).astype(jnp.float32) * (hd ** -0.5)
    qi = jnp.arange(A_BLOCK)[:, None] + A_BLOCK
    kj = jnp.arange(2 * A_BLOCK)[None, :]
    dist = qi - kj
    band = (dist >= 0) & (dist <= span)
    has_prev = (jnp.arange(nb)[:, None, None] > 0) | (kj >= A_BLOCK)[None]
    mask = band[None] & has_prev
    scores = jnp.where(mask[:, None], scores, -jnp.inf)
    m = jnp.max(scores, axis=-1, keepdims=True)
    p = jnp.exp(scores - m)
    l = jnp.sum(p, axis=-1, keepdims=True)
    o = jnp.einsum('brnhqk,brnkhd->brnqhd', (p / l).astype(v.dtype), vv)
    lse = (m + jnp.log(l))[..., 0]
    o = o.reshape(bsz, dil, nb * A_BLOCK, h, hd)[:, :, :n].transpose(0, 2, 1, 3, 4).reshape(bsz, s, h, hd)
    lse = lse.transpose(0, 1, 2, 4, 3).reshape(bsz, dil, nb * A_BLOCK, h)[:, :, :n]
    lse = lse.transpose(0, 2, 1, 3).reshape(bsz, s, h)
    return o, lse


def dilated_mixer(q, k, v):
    bsz, s, _, hd = q.shape
    outs, lses = [], []
    for g, (window, dil) in enumerate(DIL_GROUPS):
        sl = slice(g * A_HEADS_PER_GROUP, (g + 1) * A_HEADS_PER_GROUP)
        o, lse = dilated_group_attention(q[:, :, sl], k[:, :, sl], v[:, :, sl], window, dil)
        outs.append(o)
        lses.append(lse)
    alpha = jax.nn.softmax(jnp.stack(lses, axis=0), axis=0)
    o = jnp.einsum('gbsh,gbshd->bshd', alpha.astype(q.dtype), jnp.stack(outs, axis=0))
    return o.reshape(bsz, s, A_OUT_WIDTH)


def sparse_mixer(q, k, v, q_idx, k_idx, w_idx):
    bsz, s, hb, hd = q.shape
    topk = min(B_TOPK_MAX, s // 4)
    nb = s // B_QBLOCK
    idx_scale = (IDX_HEADS ** -0.5) * (IDX_DIM ** -0.5)
    key_pos = jnp.arange(s)

    def blocks(t):
        return t.reshape(bsz, nb, B_QBLOCK, *t.shape[2:]).swapaxes(0, 1)

    def one_block(args):
        qb, qib, wb, start = args
        tq = start + jnp.arange(B_QBLOCK)
        rel = jax.nn.relu(jnp.einsum('bqhd,bsd->bqhs', qib, k_idx).astype(jnp.float32))
        score = jnp.einsum('bqh,bqhs->bqs', wb.astype(jnp.float32), rel) * idx_scale
        admissible = key_pos[None, :] <= tq[:, None]
        score = jnp.where(admissible[None], score, -jnp.inf)
        _, sel = lax.top_k(score, topk)
        valid = sel <= tq[None, :, None]
        kg = jax.vmap(lambda kb, ib: kb[ib])(k, sel)
        vg = jax.vmap(lambda vb, ib: vb[ib])(v, sel)
        att = jnp.einsum('bqhd,bqkd->bqhk', qb, kg).astype(jnp.float32) * (hd ** -0.5)
        att = jnp.where(valid[:, :, None, :], att, -jnp.inf)
        p = jax.nn.softmax(att, axis=-1)
        return jnp.einsum('bqhk,bqkd->bqhd', p.astype(v.dtype), vg)

    starts = jnp.arange(nb) * B_QBLOCK
    o = lax.map(one_block, (blocks(q), blocks(q_idx), blocks(w_idx), starts))
    return o.swapaxes(0, 1).reshape(bsz, s, hb * hd)


def rglru_mixer(xc, yc, conv_w, conv_b, wa, ba, wx, bx, lam):
    bsz, s, dr = xc.shape
    xc = causal_dwconv(xc, conv_w, conv_b)
    xb = xc.reshape(bsz, s, LRU_BLOCKS, LRU_BLOCK_W)
    r = jax.nn.sigmoid((jnp.einsum('bshi,hij->bshj', xb, wa).reshape(bsz, s, dr) + ba).astype(jnp.float32))
    gate_i = jax.nn.sigmoid((jnp.einsum('bshi,hij->bshj', xb, wx).reshape(bsz, s, dr) + bx).astype(jnp.float32))
    log_a = -LRU_C * r * jax.nn.softplus(-lam.astype(jnp.float32))
    a = jnp.exp(log_a)
    b = jnp.sqrt(-jnp.expm1(2.0 * log_a)) * gate_i * xc.astype(jnp.float32)

    def combine(c1, c2):
        a1, b1 = c1
        a2, b2 = c2
        return a1 * a2, a2 * b1 + b2

    _, h = lax.associative_scan(combine, (a, b), axis=1)
    return h.astype(xc.dtype) * jax.nn.gelu(yc)


def temporal_block(x, positions, w_in, b_gate, qn_a, kn_a, qn_b, kn_b, lru_conv_w, lru_conv_b,
                   lru_wa, lru_ba, lru_wx, lru_bx, lru_lambda, w_proj_a, w_proj_b, w_proj_c, w_out, norm_mix):
    bsz, s, _ = x.shape
    h = rms_norm(x, norm_mix)
    qa, ka, va, qb, kb, vb, qi, ki, wi, xc, yc, gates = split_columns(h @ w_in, IN_SPLITS)

    def heads(t, n):
        return t.reshape(bsz, s, n, t.shape[-1] // n)

    qa = rope_partial(rms_norm(heads(qa, A_HEADS), qn_a), positions, ROT_DIM)
    ka = rope_partial(rms_norm(heads(ka, A_HEADS), kn_a), positions, ROT_DIM)
    va = heads(va, A_HEADS)
    qb = rope_partial(rms_norm(heads(qb, B_HEADS), qn_b), positions, ROT_DIM)
    kb = rope_partial(rms_norm(heads(kb, 1), kn_b), positions, ROT_DIM)[:, :, 0]
    qi = rope_partial(heads(qi, IDX_HEADS), positions, IDX_ROT)
    ki = rope_partial(heads(ki, 1), positions, IDX_ROT)[:, :, 0]

    out_a = dilated_mixer(qa, ka, va)
    out_b = sparse_mixer(qb, kb, vb, qi, ki, wi)
    out_c = rglru_mixer(xc, yc, lru_conv_w, lru_conv_b, lru_wa, lru_ba, lru_wx, lru_bx, lru_lambda)

    g = jax.nn.sigmoid((gates.reshape(bsz, s, N_BRANCH, -1) + b_gate).astype(jnp.float32)).astype(x.dtype)
    merged = (g[:, :, 0] * (out_a @ w_proj_a)
              + g[:, :, 1] * (out_b @ w_proj_b)
              + g[:, :, 2] * (out_c @ w_proj_c))
    return x + merged @ w_out


def conv_ffn_block(x, norm_ffn, ffn_w_up, ffn_conv_w, ffn_conv_b, ffn_w_down):
    h = rms_norm(x, norm_ffn)
    u = causal_dwconv(h @ ffn_w_up, ffn_conv_w, ffn_conv_b)
    ua, ub = jnp.split(u, 2, axis=-1)
    return x + (jax.nn.gelu(ua) * ub) @ ffn_w_down


def setup_inputs(seed: int = 0) -> dict:
    key = jax.random.key(seed)
    ks = jax.random.split(key, 32)
    f32 = jnp.float32

    def dense(k, shape, fan_in):
        return jax.random.normal(k, shape, f32) * (fan_in ** -0.5)

    def gain(k, shape):
        return 1.0 + 0.02 * jax.random.normal(k, shape, f32)

    def small(k, shape):
        return 0.01 * jax.random.normal(k, shape, f32)

    x = jax.random.normal(ks[0], (BATCH, SEQ, D_MODEL), f32)
    offset = jax.random.randint(ks[1], (BATCH, 1), 0, 4096, dtype=jnp.int32)
    positions = offset + jnp.arange(SEQ, dtype=jnp.int32)[None, :]
    u = jax.random.uniform(ks[2], (DEPTH, LRU_WIDTH), f32, 0.9, 0.999)
    a0 = u ** (1.0 / LRU_C)
    lru_lambda = jnp.log(a0) - jnp.log1p(-a0)
    return {
        'x': x,
        'positions': positions,
        'w_in': dense(ks[3], (DEPTH, D_MODEL, IN_COLS), D_MODEL),
        'b_gate': small(ks[4], (DEPTH, N_BRANCH, D_MODEL)),
        'qn_a': gain(ks[5], (DEPTH, HEAD_DIM)),
        'kn_a': gain(ks[6], (DEPTH, HEAD_DIM)),
        'qn_b': gain(ks[7], (DEPTH, HEAD_DIM)),
        'kn_b': gain(ks[8], (DEPTH, HEAD_DIM)),
        'lru_conv_w': dense(ks[9], (DEPTH, LRU_CONV_W, LRU_WIDTH), LRU_CONV_W),
        'lru_conv_b': small(ks[10], (DEPTH, LRU_WIDTH)),
        'lru_wa': dense(ks[11], (DEPTH, LRU_BLOCKS, LRU_BLOCK_W, LRU_BLOCK_W), LRU_BLOCK_W),
        'lru_ba': small(ks[12], (DEPTH, LRU_WIDTH)),
        'lru_wx': dense(ks[13], (DEPTH, LRU_BLOCKS, LRU_BLOCK_W, LRU_BLOCK_W), LRU_BLOCK_W),
        'lru_bx': small(ks[14], (DEPTH, LRU_WIDTH)),
        'lru_lambda': lru_lambda,
        'w_proj_a': dense(ks[15], (DEPTH, A_OUT_WIDTH, D_MODEL), A_OUT_WIDTH),
        'w_proj_b': dense(ks[16], (DEPTH, B_WIDTH, D_MODEL), B_WIDTH),
        'w_proj_c': dense(ks[17], (DEPTH, LRU_WIDTH, D_MODEL), LRU_WIDTH),
        'w_out': dense(ks[18], (DEPTH, D_MODEL, D_MODEL), D_MODEL),
        'norm_mix': gain(ks[19], (DEPTH, D_MODEL)),
        'norm_ffn': gain(ks[20], (DEPTH, D_MODEL)),
        'ffn_w_up': dense(ks[21], (DEPTH, D_MODEL, 2 * D_FF), D_MODEL),
        'ffn_conv_w': dense(ks[22], (DEPTH, FFN_CONV_W, 2 * D_FF), FFN_CONV_W),
        'ffn_conv_b': small(ks[23], (DEPTH, 2 * D_FF)),
        'ffn_w_down': dense(ks[24], (DEPTH, D_FF, D_MODEL), D_FF),
    }


def reference(x, positions, w_in, b_gate, qn_a, kn_a, qn_b, kn_b, lru_conv_w, lru_conv_b,
              lru_wa, lru_ba, lru_wx, lru_bx, lru_lambda, w_proj_a, w_proj_b, w_proj_c, w_out,
              norm_mix, norm_ffn, ffn_w_up, ffn_conv_w, ffn_conv_b, ffn_w_down):
    for layer in range(DEPTH):
        x = temporal_block(x, positions, w_in[layer], b_gate[layer], qn_a[layer], kn_a[layer],
                           qn_b[layer], kn_b[layer], lru_conv_w[layer], lru_conv_b[layer],
                           lru_wa[layer], lru_ba[layer], lru_wx[layer], lru_bx[layer], lru_lambda[layer],
                           w_proj_a[layer], w_proj_b[layer], w_proj_c[layer], w_out[layer], norm_mix[layer])
        x = conv_ffn_block(x, norm_ffn[layer], ffn_w_up[layer], ffn_conv_w[layer], ffn_conv_b[layer],
                           ffn_w_down[layer])
    return x
```

```python
import functools
import math

import numpy as np
import jax
import jax.numpy as jnp
from jax import lax
from jax.experimental import pallas as pl
from jax.experimental.pallas import tpu as pltpu

F32 = jnp.float32
MXU_DTYPE = jnp.bfloat16

LANES = 128
SUBLANES = 8
VMEM_LIMIT_BYTES = 58 * 1024 * 1024

HEAD_DIM = 128
ROPE_THETA = 500000.0
ROT_DIM = HEAD_DIM // 4
NORM_EPS = 1e-6
DIL_GROUPS = ((128, 1), (512, 4), (2048, 16))
A_HEADS_PER_GROUP = 4
A_HEADS = A_HEADS_PER_GROUP * len(DIL_GROUPS)
A_BLOCK = 128
B_HEADS = 12
B_TOPK_MAX = 256
IDX_HEADS = 16
IDX_DIM = 64
IDX_ROT = IDX_DIM // 4
LRU_BLOCKS = 16
LRU_C = 8.0
LRU_CONV_W = 4
FFN_CONV_W = 3
N_BRANCH = 3
A_GROUP_WIDTH = A_HEADS_PER_GROUP * HEAD_DIM

QK_COLS = 5120
V_COLS = 2048
IDX_COLS = 1280
QB_BLOCK = 128
KEY_TILE = 256
INT_MIN = -(2 ** 31)
NEG_BIG = -1e30


def _params(*semantics):
    return pltpu.CompilerParams(dimension_semantics=semantics,
                                vmem_limit_bytes=VMEM_LIMIT_BYTES)


def _gelu_tanh(x):
    return 0.5 * x * (1.0 + jnp.tanh(0.7978845608028654 * (x + 0.044715 * (x * x * x))))


def _sigmoid(x):
    return 1.0 / (1.0 + jnp.exp(-x))


def _rope_tables_kernel(pos_ref, par_ref, c_ref, s1_ref, s2_ref):
    ang = pos_ref[...] * par_ref[0:1, :]
    c = jnp.cos(ang)
    s = jnp.sin(ang)
    c_ref[...] = jnp.where(par_ref[1:2, :] > 0.0, c, 1.0)
    s1_ref[...] = -s * par_ref[2:3, :]
    s2_ref[...] = s * par_ref[3:4, :]


def _rope_tables(pos_col, rot_dim, period):
    t = pos_col.shape[0]
    half = rot_dim // 2
    freq = ROPE_THETA ** (-jnp.arange(half, dtype=F32) / half)
    lane = np.arange(LANES) % period
    in_rot = lane < rot_dim
    freq_lane = jnp.where(jnp.asarray(in_rot), freq[lane % half], 0.0)
    rows = [freq_lane,
            jnp.asarray(in_rot, F32),
            jnp.asarray(lane < half, F32),
            jnp.asarray((lane >= half) & in_rot, F32)]
    par = jnp.stack(rows + [jnp.zeros((LANES,), F32)] * 4, axis=0)
    tm = min(t, 2048)
    spec = pl.BlockSpec((tm, LANES), lambda i: (i, 0))
    return pl.pallas_call(
        _rope_tables_kernel,
        grid=(t // tm,),
        in_specs=[pl.BlockSpec((tm, 1), lambda i: (i, 0)),
                  pl.BlockSpec((SUBLANES, LANES), lambda i: (0, 0))],
        out_specs=[spec, spec, spec],
        out_shape=[jax.ShapeDtypeStruct((t, LANES), F32)] * 3,
        compiler_params=_params("parallel"),
        name="rope_tables",
    )(pos_col, par)


def _norm_proj_kernel(n_rows, n_cols, epilogue, x_ref, g_ref, w_ref, *rest):
    row_refs = rest[:n_rows]
    col_refs = rest[n_rows:n_rows + n_cols]
    o_ref, h_ref = rest[n_rows + n_cols:]

    @pl.when(pl.program_id(1) == 0)
    def _():
        xf = x_ref[...]
        ms = jnp.mean(xf * xf, axis=-1, keepdims=True)
        h_ref[...] = (xf * lax.rsqrt(ms + NORM_EPS) * g_ref[...]).astype(h_ref.dtype)

    acc = jnp.dot(h_ref[...], w_ref[...], preferred_element_type=F32)
    if epilogue is not None:
        acc = epilogue(acc, [r[...] for r in row_refs], [c[...] for c in col_refs])
    o_ref[...] = acc.astype(o_ref.dtype)


def _norm_proj(x2d, gain, w, *, tn, out_dtype, name, epilogue=None, row_inputs=(), col_inputs=()):
    t, d = x2d.shape
    n = w.shape[1]
    tm = min(t, 1024)
    in_specs = [pl.BlockSpec((tm, d), lambda i, j: (i, 0)),
                pl.BlockSpec((1, d), lambda i, j: (0, 0)),
                pl.BlockSpec((d, tn), lambda i, j: (0, j))]
    in_specs += [pl.BlockSpec((tm, LANES), lambda i, j: (i, 0)) for _ in row_inputs]
    in_specs += [pl.BlockSpec((1, tn), lambda i, j: (0, j)) for _ in col_inputs]
    return pl.pallas_call(
        functools.partial(_norm_proj_kernel, len(row_inputs), len(col_inputs), epilogue),
        grid=(t // tm, n // tn),
        in_specs=in_specs,
        out_specs=pl.BlockSpec((tm, tn), lambda i, j: (i, j)),
        out_shape=jax.ShapeDtypeStruct((t, n), out_dtype),
        scratch_shapes=[pltpu.VMEM((tm, d), MXU_DTYPE)],
        compiler_params=_params("parallel", "arbitrary"),
        name=name,
    )(x2d, gain, w, *row_inputs, *col_inputs)


def _rope_lanes(y, tables, half):
    c, s1, s2 = tables
    return (y * c + pltpu.roll(y, LANES - half, 1) * s1 + pltpu.roll(y, half, 1) * s2)


def _ep_headnorm_rope(acc, rows, cols):
    gain, = cols
    outs = []
    for c in range(acc.shape[1] // LANES):
        y = acc[:, c * LANES:(c + 1) * LANES]
        ms = jnp.mean(y * y, axis=-1, keepdims=True)
        y = y * lax.rsqrt(ms + NORM_EPS) * gain[:, c * LANES:(c + 1) * LANES]
        outs.append(_rope_lanes(y, rows, ROT_DIM // 2))
    return jnp.concatenate(outs, axis=1)


def _ep_rope_idx(acc, rows, cols):
    outs = []
    for c in range(acc.shape[1] // LANES):
        outs.append(_rope_lanes(acc[:, c * LANES:(c + 1) * LANES], rows, IDX_ROT // 2))
    return jnp.concatenate(outs, axis=1)


def _dilated_kernel(span, q_ref, kc_ref, kp_ref, vc_ref, vp_ref, o_ref, lse_ref):
    i = pl.program_id(2)
    qi = lax.broadcasted_iota(jnp.int32, (A_BLOCK, 2 * A_BLOCK), 0) + A_BLOCK
    kj = lax.broadcasted_iota(jnp.int32, (A_BLOCK, 2 * A_BLOCK), 1)
    dist = qi - kj
    min_key = jnp.where(i > 0, 0, A_BLOCK)
    mask = (dist >= 0) & (dist <= span) & (kj >= min_key)
    for h in range(A_HEADS_PER_GROUP):
        hs = slice(h * HEAD_DIM, (h + 1) * HEAD_DIM)
        q = q_ref[:, hs]
        k2 = jnp.concatenate([kp_ref[:, hs], kc_ref[:, hs]], axis=0)
        v2 = jnp.concatenate([vp_ref[:, hs], vc_ref[:, hs]], axis=0)
        s = lax.dot_general(q, k2, (((1,), (1,)), ((), ())), preferred_element_type=F32)
        s = jnp.where(mask, s, NEG_BIG)
        m = jnp.max(s, axis=-1, keepdims=True)
        p = jnp.exp(s - m)
        l = jnp.sum(p, axis=-1, keepdims=True)
        o = jnp.dot(p.astype(v2.dtype), v2, preferred_element_type=F32)
        o_ref[:, hs] = o / l
        lse_ref[:, hs] = jnp.broadcast_to(m + jnp.log(l), (A_BLOCK, HEAD_DIM))


def _dilated_group(qk, v, bsz, s, g, window, dil):
    n = s // dil
    nb = n // A_BLOCK
    cq = QK_COLS // A_GROUP_WIDTH
    cv = V_COLS // A_GROUP_WIDTH
    qk3 = qk.reshape(bsz, n, dil * QK_COLS)
    v3 = v.reshape(bsz, n, dil * V_COLS)
    blk = (None, A_BLOCK, A_GROUP_WIDTH)
    k_off = A_HEADS // A_HEADS_PER_GROUP

    def prev(i):
        return jnp.maximum(i - 1, 0)

    in_specs = [
        pl.BlockSpec(blk, lambda b, r, i: (b, i, r * cq + g)),
        pl.BlockSpec(blk, lambda b, r, i: (b, i, r * cq + k_off + g)),
        pl.BlockSpec(blk, lambda b, r, i: (b, prev(i), r * cq + k_off + g)),
        pl.BlockSpec(blk, lambda b, r, i: (b, i, r * cv + g)),
        pl.BlockSpec(blk, lambda b, r, i: (b, prev(i), r * cv + g)),
    ]
    out_spec = pl.BlockSpec(blk, lambda b, r, i: (b, i, r))
    out_sds = jax.ShapeDtypeStruct((bsz, n, dil * A_GROUP_WIDTH), F32)
    o, lse = pl.pallas_call(
        functools.partial(_dilated_kernel, window // dil),
        grid=(bsz, dil, nb),
        in_specs=in_specs,
        out_specs=[out_spec, out_spec],
        out_shape=[out_sds, out_sds],
        compiler_params=_params("parallel", "parallel", "arbitrary"),
        name=f"dilated_attn_g{g}",
    )(qk3, qk3, qk3, v3, v3)
    return o.reshape(bsz * s, A_GROUP_WIDTH), lse.reshape(bsz * s, A_GROUP_WIDTH)


def _merge_groups_kernel(o0, o1, o2, l0, l1, l2, out_ref):
    a, b, c = l0[...], l1[...], l2[...]
    m = jnp.maximum(jnp.maximum(a, b), c)
    ea, eb, ec = jnp.exp(a - m), jnp.exp(b - m), jnp.exp(c - m)
    num = ea * o0[...] + eb * o1[...] + ec * o2[...]
    out_ref[...] = (num / (ea + eb + ec)).astype(out_ref.dtype)


def _merge_groups(os_, lses):
    t = os_[0].shape[0]
    tm = min(t, 1024)
    spec = pl.BlockSpec((tm, A_GROUP_WIDTH), lambda i: (i, 0))
    return pl.pallas_call(
        _merge_groups_kernel,
        grid=(t // tm,),
        in_specs=[spec] * 6,
        out_specs=spec,
        out_shape=jax.ShapeDtypeStruct((t, A_GROUP_WIDTH), MXU_DTYPE),
        compiler_params=_params("parallel"),
        name="dilated_merge",
    )(*os_, *lses)


def _sparse_kernel(topk, idx_scale, qit_ref, wt_ref, ki_ref, qbt_ref, kb_ref, vt_ref,
                   out_ref, keys_ref, acc_ref, m_ref):
    qb = pl.program_id(1)
    t0 = qb * QB_BLOCK
    n_tiles = (t0 + QB_BLOCK + KEY_TILE - 1) // KEY_TILE
    lane_q = lax.broadcasted_iota(jnp.int32, (KEY_TILE, QB_BLOCK), 1) + t0
    row_id = lax.broadcasted_iota(jnp.int32, (KEY_TILE, QB_BLOCK), 0)

    def tile_start(kt):
        return pl.multiple_of(kt * KEY_TILE, KEY_TILE)

    def score_tile(kt, carry):
        r0 = tile_start(kt)
        k_tile = ki_ref[pl.ds(r0, KEY_TILE), 0:IDX_DIM]
        acc = jnp.zeros((KEY_TILE, QB_BLOCK), F32)
        for h in range(IDX_HEADS):
            s = jnp.dot(k_tile, qit_ref[:, h * QB_BLOCK:(h + 1) * QB_BLOCK],
                        preferred_element_type=F32)
            acc = acc + jnp.maximum(s, 0.0) * (wt_ref[h:h + 1, :] * idx_scale)
        bits = pltpu.bitcast(acc, jnp.int32)
        key = bits ^ ((bits >> 31) & 0x7FFFFFFF)
        key = jnp.where(acc == 0.0, 0, key)
        key = jnp.where(row_id + r0 <= lane_q, key, INT_MIN)
        keys_ref[pl.ds(r0, KEY_TILE), :] = key
        return carry

    lax.fori_loop(0, n_tiles, score_tile, 0)

    def count_where(pred):
        def body(kt, acc):
            r0 = tile_start(kt)
            hit = jnp.where(pred(keys_ref[pl.ds(r0, KEY_TILE), :], r0), 1, 0)
            return acc + jnp.sum(hit.reshape(KEY_TILE // SUBLANES, SUBLANES, QB_BLOCK), axis=0)
        acc = lax.fori_loop(0, n_tiles, body, jnp.zeros((SUBLANES, QB_BLOCK), jnp.int32))
        return jnp.sum(acc, axis=0, keepdims=True)

    def count_ge(cand):
        return count_where(lambda k, r0: k >= cand)

    first = jnp.zeros((1, QB_BLOCK), jnp.int32)
    tau = jnp.where(count_ge(first) >= topk, first, jnp.full((1, QB_BLOCK), INT_MIN, jnp.int32))

    def bit_step(i, tau):
        cand = tau | lax.shift_left(jnp.int32(1), 30 - i)
        return jnp.where(count_ge(cand) >= topk, cand, tau)

    tau = lax.fori_loop(0, 31, bit_step, tau)
    tau = jnp.maximum(tau, INT_MIN + 1)
    n_ge = count_ge(tau)

    @pl.when(jnp.max(n_ge) > topk)
    def _():
        need = topk - count_ge(tau + 1)

        def idx_step(i, last):
            cand = last | lax.shift_left(jnp.int32(1), 30 - i)
            below = count_where(lambda k, r0: (k == tau) & (row_id + r0 < cand))
            return jnp.where(below < need, cand, last)

        last = lax.fori_loop(0, 31, idx_step, jnp.zeros((1, QB_BLOCK), jnp.int32))

        def demote(kt, carry):
            r0 = tile_start(kt)
            k = keys_ref[pl.ds(r0, KEY_TILE), :]
            drop = (k == tau) & (row_id + r0 > last)
            keys_ref[pl.ds(r0, KEY_TILE), :] = jnp.where(drop, tau - 1, k)
            return carry

        lax.fori_loop(0, n_tiles, demote, 0)

    m_ref[...] = jnp.full(m_ref.shape, NEG_BIG, F32)
    acc_ref[...] = jnp.zeros(acc_ref.shape, F32)

    def attend_tile(kt, carry):
        r0 = tile_start(kt)
        sel = keys_ref[pl.ds(r0, KEY_TILE), :] >= tau
        s_all = jnp.dot(kb_ref[pl.ds(r0, KEY_TILE), :], qbt_ref[...], preferred_element_type=F32)
        ps = []
        alphas = []
        for h in range(B_HEADS):
            hs = slice(h * QB_BLOCK, (h + 1) * QB_BLOCK)
            s = jnp.where(sel, s_all[:, hs], NEG_BIG)
            m_old = m_ref[:, hs]
            m_new = jnp.maximum(m_old, jnp.max(s, axis=0, keepdims=True))
            m_ref[:, hs] = m_new
            alphas.append(jnp.exp(m_old - m_new))
            ps.append(jnp.exp(s - m_new).astype(MXU_DTYPE))
        p_all = jnp.concatenate(ps, axis=1)
        alpha = jnp.concatenate(alphas, axis=1)
        pv = jnp.dot(vt_ref[:, pl.ds(r0, KEY_TILE)], p_all, preferred_element_type=F32)
        acc_ref[...] = acc_ref[...] * alpha + pv
        return carry

    lax.fori_loop(0, n_tiles, attend_tile, 0)

    for h in range(B_HEADS):
        hs = slice(h * QB_BLOCK, (h + 1) * QB_BLOCK)
        o_t = acc_ref[0:HEAD_DIM, hs] / acc_ref[HEAD_DIM:HEAD_DIM + 1, hs]
        out_ref[:, h * HEAD_DIM:(h + 1) * HEAD_DIM] = o_t.T.astype(out_ref.dtype)


def _sparse_mixer(qk, v, idx, wi_t, bsz, s):
    nqb = s // QB_BLOCK
    topk = min(B_TOPK_MAX, s // 4)
    idx_scale = (IDX_HEADS ** -0.5) * (IDX_DIM ** -0.5)
    nq = B_HEADS * QB_BLOCK
    qi = idx[:, :IDX_HEADS * IDX_DIM].reshape(bsz, nqb, QB_BLOCK, IDX_HEADS, IDX_DIM)
    qit = qi.transpose(0, 1, 4, 3, 2).reshape(bsz, nqb, IDX_DIM, IDX_HEADS * QB_BLOCK)
    qb_off = 2 * A_HEADS * HEAD_DIM
    qbq = qk[:, qb_off:qb_off + B_HEADS * HEAD_DIM].reshape(bsz, nqb, QB_BLOCK, B_HEADS, HEAD_DIM)
    qbt = qbq.transpose(0, 1, 4, 3, 2).reshape(bsz, nqb, HEAD_DIM, nq)
    vb = v[:, A_HEADS * HEAD_DIM:A_HEADS * HEAD_DIM + HEAD_DIM].reshape(bsz, s, HEAD_DIM)
    vt = jnp.concatenate([vb.transpose(0, 2, 1),
                          jnp.ones((bsz, 1, s), vb.dtype),
                          jnp.zeros((bsz, SUBLANES - 1, s), vb.dtype)], axis=1)
    v_rows = HEAD_DIM + SUBLANES
    qk3 = qk.reshape(bsz, s, QK_COLS)
    idx3 = idx.reshape(bsz, s, IDX_COLS)
    kb_blk = (qb_off + B_HEADS * HEAD_DIM) // LANES
    ki_blk = (IDX_HEADS * IDX_DIM) // LANES
    out = pl.pallas_call(
        functools.partial(_sparse_kernel, topk, idx_scale),
        grid=(bsz, nqb),
        in_specs=[
            pl.BlockSpec((None, None, IDX_DIM, IDX_HEADS * QB_BLOCK), lambda b, q: (b, q, 0, 0)),
            pl.BlockSpec((None, None, IDX_HEADS, QB_BLOCK), lambda b, q: (b, q, 0, 0)),
            pl.BlockSpec((None, s, LANES), lambda b, q: (b, 0, ki_blk)),
            pl.BlockSpec((None, None, HEAD_DIM, nq), lambda b, q: (b, q, 0, 0)),
            pl.BlockSpec((None, s, LANES), lambda b, q: (b, 0, kb_blk)),
            pl.BlockSpec((None, v_rows, s), lambda b, q: (b, 0, 0)),
        ],
        out_specs=pl.BlockSpec((None, QB_BLOCK, B_HEADS * HEAD_DIM), lambda b, q: (b, q, 0)),
        out_shape=jax.ShapeDtypeStruct((bsz, s, B_HEADS * HEAD_DIM), MXU_DTYPE),
        scratch_shapes=[pltpu.VMEM((s, QB_BLOCK), jnp.int32),
                        pltpu.VMEM((v_rows, nq), F32),
                        pltpu.VMEM((1, nq), F32)],
        compiler_params=_params("parallel", "arbitrary"),
        name="sparse_mixer",
    )(qit, wi_t, idx3, qbt, qk3, vt)
    return out.reshape(bsz * s, B_HEADS * HEAD_DIM)


def _lru_kernel(ts, xc_ref, yc_ref, cw_ref, cb_ref, wa_ref, wx_ref, ba_ref, bx_ref, lam_ref,
                o_ref, xbuf, a_buf, b_buf, h_state):
    halo = SUBLANES

    @pl.when(pl.program_id(1) == 0)
    def _():
        xbuf[0:halo, :] = jnp.zeros((halo, xbuf.shape[1]), F32)
        h_state[...] = jnp.zeros(h_state.shape, F32)

    xbuf[halo:halo + ts, :] = xc_ref[...]
    xc = cb_ref[...] + cw_ref[0:1, :] * xbuf[halo - 3:halo - 3 + ts, :]
    for tap in range(1, LRU_CONV_W):
        lo = halo - (LRU_CONV_W - 1) + tap
        xc = xc + cw_ref[tap:tap + 1, :] * xbuf[lo:lo + ts, :]
    xbuf[0:halo, :] = xc_ref[ts - halo:ts, :]

    z = -lam_ref[...]
    softplus = jnp.maximum(z, 0.0) + jnp.log1p(jnp.exp(-jnp.abs(z)))
    bw = xc.shape[1] // LRU_BLOCKS
    for blk in range(LRU_BLOCKS):
        cs = slice(blk * bw, (blk + 1) * bw)
        xs = xc[:, cs]
        xs_m = xs.astype(MXU_DTYPE)
        r = _sigmoid(jnp.dot(xs_m, wa_ref[blk], preferred_element_type=F32) + ba_ref[:, cs])
        gate_i = _sigmoid(jnp.dot(xs_m, wx_ref[blk], preferred_element_type=F32) + bx_ref[:, cs])
        log_a = (-LRU_C) * r * softplus[:, cs]
        a = jnp.exp(log_a)
        a_buf[:, cs] = a
        b_buf[:, cs] = jnp.sqrt(-jnp.tanh(log_a) * (a * a + 1.0)) * gate_i * xs

    def step(t, h):
        h = a_buf[pl.ds(t, 1), :] * h + b_buf[pl.ds(t, 1), :]
        b_buf[pl.ds(t, 1), :] = h
        return h

    h_state[0:1, :] = lax.fori_loop(0, ts, step, h_state[0:1, :], unroll=8)
    o_ref[...] = (b_buf[...] * _gelu_tanh(yc_ref[...])).astype(o_ref.dtype)


def _lru_mixer(feat, bsz, s, width, conv_w, conv_b, wa, ba, wx, bx, lam):
    ts = min(s, 512)
    nt = s // ts
    row = lambda a: a.reshape(1, width).astype(F32)
    vec = pl.BlockSpec((1, width), lambda b, t: (0, 0))
    wspec = pl.BlockSpec(wa.shape, lambda b, t: (0, 0, 0))
    return pl.pallas_call(
        functools.partial(_lru_kernel, ts),
        grid=(bsz, nt),
        in_specs=[pl.BlockSpec((ts, width), lambda b, t: (b * nt + t, 0)),
                  pl.BlockSpec((ts, width), lambda b, t: (b * nt + t, 1)),
                  pl.BlockSpec((LRU_CONV_W, width), lambda b, t: (0, 0)),
                  vec, wspec, wspec, vec, vec, vec],
        out_specs=pl.BlockSpec((ts, width), lambda b, t: (b * nt + t, 0)),
        out_shape=jax.ShapeDtypeStruct((bsz * s, width), MXU_DTYPE),
        scratch_shapes=[pltpu.VMEM((ts + SUBLANES, width), F32),
                        pltpu.VMEM((ts, width), F32),
                        pltpu.VMEM((ts, width), F32),
                        pltpu.VMEM((SUBLANES, width), F32)],
        compiler_params=_params("parallel", "arbitrary"),
        name="rglru_mixer",
    )(feat, feat, conv_w.astype(F32), row(conv_b), wa.astype(MXU_DTYPE), wx.astype(MXU_DTYPE),
      row(ba), row(bx), row(lam))


def _gate_merge_kernel(oa, ob, oc, wa, wb, wc, g0, g1, g2, b0, b1, b2, out_ref):
    acc = _sigmoid(g0[...] + b0[...]) * jnp.dot(oa[...], wa[...], preferred_element_type=F32)
    acc = acc + _sigmoid(g1[...] + b1[...]) * jnp.dot(ob[...], wb[...], preferred_element_type=F32)
    acc = acc + _sigmoid(g2[...] + b2[...]) * jnp.dot(oc[...], wc[...], preferred_element_type=F32)
    out_ref[...] = acc.astype(out_ref.dtype)


def _gate_merge(out_a, out_b, out_c, wpa, wpb, wpc, feat, gate_col0, b_gate_row, d_model):
    t = out_a.shape[0]
    tm = min(t, 1024)
    tn = 512
    nj = d_model // tn
    g_blk = gate_col0 // tn

    def act(a):
        return pl.BlockSpec((tm, a.shape[1]), lambda i, j: (i, 0))

    def wgt(w):
        return pl.BlockSpec((w.shape[0], tn), lambda i, j: (0, j))

    gates = [pl.BlockSpec((tm, tn), functools.partial(lambda br, i, j: (i, g_blk + br * nj + j), br))
             for br in range(N_BRANCH)]
    biases = [pl.BlockSpec((1, tn), functools.partial(lambda br, i, j: (0, br * nj + j), br))
              for br in range(N_BRANCH)]
    return pl.pallas_call(
        _gate_merge_kernel,
        grid=(t // tm, nj),
        in_specs=[act(out_a), act(out_b), act(out_c), wgt(wpa), wgt(wpb), wgt(wpc)] + gates + biases,
        out_specs=pl.BlockSpec((tm, tn), lambda i, j: (i, j)),
        out_shape=jax.ShapeDtypeStruct((t, d_model), MXU_DTYPE),
        compiler_params=_params("parallel", "arbitrary"),
        name="gate_merge",
    )(out_a, out_b, out_c, wpa, wpb, wpc, feat, feat, feat, b_gate_row, b_gate_row, b_gate_row)


def _matmul_residual_kernel(a_ref, w_ref, r_ref, o_ref):
    o_ref[...] = r_ref[...] + jnp.dot(a_ref[...], w_ref[...], preferred_element_type=F32)


def _matmul_residual(a, w, resid, name):
    t, k = a.shape
    n = w.shape[1]
    tm = min(t, 1024)
    tn = 512
    return pl.pallas_call(
        _matmul_residual_kernel,
        grid=(t // tm, n // tn),
        in_specs=[pl.BlockSpec((tm, k), lambda i, j: (i, 0)),
                  pl.BlockSpec((k, tn), lambda i, j: (0, j)),
                  pl.BlockSpec((tm, tn), lambda i, j: (i, j))],
        out_specs=pl.BlockSpec((tm, tn), lambda i, j: (i, j)),
        out_shape=jax.ShapeDtypeStruct((t, n), F32),
        compiler_params=_params("parallel", "arbitrary"),
        name=name,
    )(a, w, resid)


def _ffn_up_kernel(tm, tiles_per_seq, x_ref, xh_ref, g_ref, wa_ref, wb_ref, cwa_ref, cwb_ref,
                   cba_ref, cbb_ref, o_ref, h_ref, hh_ref, ua_buf, ub_buf):
    halo = SUBLANES

    def normed(ref):
        xf = ref[...]
        ms = jnp.mean(xf * xf, axis=-1, keepdims=True)
        return (xf * lax.rsqrt(ms + NORM_EPS) * g_ref[...]).astype(MXU_DTYPE)

    @pl.when(pl.program_id(1) == 0)
    def _():
        h_ref[...] = normed(x_ref)
        hh_ref[...] = normed(xh_ref)

    keep = jnp.where(pl.program_id(0) % tiles_per_seq == 0, 0.0, 1.0)

    def conv_half(w_ref, cw_ref, cb_ref, buf):
        buf[0:halo, :] = keep * jnp.dot(hh_ref[...], w_ref[...], preferred_element_type=F32)
        buf[halo:halo + tm, :] = jnp.dot(h_ref[...], w_ref[...], preferred_element_type=F32)
        y = cb_ref[...] + cw_ref[0:1, :] * buf[halo - 2:halo - 2 + tm, :]
        for tap in range(1, FFN_CONV_W):
            lo = halo - (FFN_CONV_W - 1) + tap
            y = y + cw_ref[tap:tap + 1, :] * buf[lo:lo + tm, :]
        return y

    ya = conv_half(wa_ref, cwa_ref, cba_ref, ua_buf)
    yb = conv_half(wb_ref, cwb_ref, cbb_ref, ub_buf)
    o_ref[...] = (_gelu_tanh(ya) * yb).astype(o_ref.dtype)


def _ffn_up(x2d, s, gain, w_up, conv_w, conv_b_row):
    t, d = x2d.shape
    d_ff = w_up.shape[1] // 2
    tm = min(s, 1024)
    tn = 512
    nj = d_ff // tn
    hb = tm // SUBLANES
    return pl.pallas_call(
        functools.partial(_ffn_up_kernel, tm, s // tm),
        grid=(t // tm, nj),
        in_specs=[pl.BlockSpec((tm, d), lambda i, j: (i, 0)),
                  pl.BlockSpec((SUBLANES, d), lambda i, j: (jnp.maximum(i * hb - 1, 0), 0)),
                  pl.BlockSpec((1, d), lambda i, j: (0, 0)),
                  pl.BlockSpec((d, tn), lambda i, j: (0, j)),
                  pl.BlockSpec((d, tn), lambda i, j: (0, nj + j)),
                  pl.BlockSpec((FFN_CONV_W, tn), lambda i, j: (0, j)),
                  pl.BlockSpec((FFN_CONV_W, tn), lambda i, j: (0, nj + j)),
                  pl.BlockSpec((1, tn), lambda i, j: (0, j)),
                  pl.BlockSpec((1, tn), lambda i, j: (0, nj + j))],
        out_specs=pl.BlockSpec((tm, tn), lambda i, j: (i, j)),
        out_shape=jax.ShapeDtypeStruct((t, d_ff), MXU_DTYPE),
        scratch_shapes=[pltpu.VMEM((tm, d), MXU_DTYPE),
                        pltpu.VMEM((SUBLANES, d), MXU_DTYPE),
                        pltpu.VMEM((tm + SUBLANES, tn), F32),
                        pltpu.VMEM((tm + SUBLANES, tn), F32)],
        compiler_params=_params("parallel", "arbitrary"),
        name="ffn_up_conv_gate",
    )(x2d, x2d, gain, w_up, w_up, conv_w, conv_w, conv_b_row, conv_b_row)


def _pad_cols(parts, total, dtype):
    width = sum(p.shape[1] for p in parts)
    rows = parts[0].shape[0]
    if total > width:
        parts = list(parts) + [jnp.zeros((rows, total - width), parts[0].dtype)]
    return jnp.concatenate(parts, axis=1).astype(dtype)


def _temporal_block(x2d, bsz, s, tabs_head, tabs_idx, w_in, b_gate, qn_a, kn_a, qn_b, kn_b,
                    lru_conv_w, lru_conv_b, lru_wa, lru_ba, lru_wx, lru_bx, lru_lambda,
                    w_proj_a, w_proj_b, w_proj_c, w_out, norm_mix):
    d = x2d.shape[1]
    a_w = A_HEADS * HEAD_DIM
    b_w = B_HEADS * HEAD_DIM
    sizes = (a_w, a_w, a_w, b_w, HEAD_DIM, HEAD_DIM, IDX_HEADS * IDX_DIM, IDX_DIM, IDX_HEADS,
             d, d, N_BRANCH * d)
    offs = np.concatenate([[0], np.cumsum(sizes)])
    col = lambda k: w_in[:, offs[k]:offs[k + 1]]
    w_qa, w_ka, w_va, w_qb, w_kb, w_vb, w_qi, w_ki, w_wi, w_xc, w_yc, w_g = [col(k) for k in range(12)]

    gain_row = norm_mix.reshape(1, d).astype(F32)
    att_scale = HEAD_DIM ** -0.5
    head_gain = jnp.concatenate([
        jnp.tile(qn_a.astype(F32) * att_scale, A_HEADS), jnp.tile(kn_a.astype(F32), A_HEADS),
        jnp.tile(qn_b.astype(F32) * att_scale, B_HEADS), kn_b.astype(F32),
        jnp.ones((QK_COLS - 2 * a_w - b_w - HEAD_DIM,), F32)]).reshape(1, QK_COLS)

    qk = _norm_proj(x2d, gain_row, _pad_cols([w_qa, w_ka, w_qb, w_kb], QK_COLS, MXU_DTYPE),
                    tn=512, out_dtype=MXU_DTYPE, name="proj_qk", epilogue=_ep_headnorm_rope,
                    row_inputs=tabs_head, col_inputs=(head_gain,))
    v = _norm_proj(x2d, gain_row, _pad_cols([w_va, w_vb], V_COLS, MXU_DTYPE),
                   tn=512, out_dtype=MXU_DTYPE, name="proj_v")
    idx = _norm_proj(x2d, gain_row, _pad_cols([w_qi, w_ki], IDX_COLS, MXU_DTYPE),
                     tn=256, out_dtype=MXU_DTYPE, name="proj_idx", epilogue=_ep_rope_idx,
                     row_inputs=tabs_idx)
    gate_col0 = 2 * d
    wi_col0 = gate_col0 + N_BRANCH * d
    feat_cols = -(-(wi_col0 + IDX_HEADS) // 512) * 512
    feat = _norm_proj(x2d, gain_row, _pad_cols([w_xc, w_yc, w_g, w_wi], feat_cols, MXU_DTYPE),
                      tn=512, out_dtype=F32, name="proj_feat")

    os_, lses = [], []
    for g, (window, dil) in enumerate(DIL_GROUPS):
        o, lse = _dilated_group(qk, v, bsz, s, g, window, dil)
        os_.append(o)
        lses.append(lse)
    out_a = _merge_groups(os_, lses)

    wi = feat[:, wi_col0:wi_col0 + IDX_HEADS].reshape(bsz, s // QB_BLOCK, QB_BLOCK, IDX_HEADS)
    out_b = _sparse_mixer(qk, v, idx, wi.transpose(0, 1, 3, 2), bsz, s)

    out_c = _lru_mixer(feat, bsz, s, d, lru_conv_w, lru_conv_b, lru_wa, lru_ba, lru_wx, lru_bx,
                       lru_lambda)

    merged = _gate_merge(out_a, out_b, out_c, w_proj_a.astype(MXU_DTYPE),
                         w_proj_b.astype(MXU_DTYPE), w_proj_c.astype(MXU_DTYPE), feat, gate_col0,
                         b_gate.reshape(1, N_BRANCH * d).astype(F32), d)
    return _matmul_residual(merged, w_out.astype(MXU_DTYPE), x2d, "out_proj_residual")


def _conv_ffn_block(x2d, s, norm_ffn, ffn_w_up, ffn_conv_w, ffn_conv_b, ffn_w_down):
    d = x2d.shape[1]
    act = _ffn_up(x2d, s, norm_ffn.reshape(1, d).astype(F32), ffn_w_up.astype(MXU_DTYPE),
                  ffn_conv_w.astype(F32), ffn_conv_b.reshape(1, -1).astype(F32))
    return _matmul_residual(act, ffn_w_down.astype(MXU_DTYPE), x2d, "ffn_down_residual")


def kernel(x, positions, w_in, b_gate, qn_a, kn_a, qn_b, kn_b, lru_conv_w, lru_conv_b, lru_wa,
           lru_ba, lru_wx, lru_bx, lru_lambda, w_proj_a, w_proj_b, w_proj_c, w_out, norm_mix,
           norm_ffn, ffn_w_up, ffn_conv_w, ffn_conv_b, ffn_w_down):
    bsz, s, d = x.shape
    assert s % (DIL_GROUPS[-1][1] * A_BLOCK) == 0 and s % KEY_TILE == 0
    x2d = x.reshape(bsz * s, d)
    pos_col = positions.astype(F32).reshape(bsz * s, 1)
    tabs_head = _rope_tables(pos_col, ROT_DIM, HEAD_DIM)
    tabs_idx = _rope_tables(pos_col, IDX_ROT, IDX_DIM)
    for layer in range(w_in.shape[0]):
        x2d = _temporal_block(
            x2d, bsz, s, tabs_head, tabs_idx, w_in[layer], b_gate[layer], qn_a[layer],
            kn_a[layer], qn_b[layer], kn_b[layer], lru_conv_w[layer], lru_conv_b[layer],
            lru_wa[layer], lru_ba[layer], lru_wx[layer], lru_bx[layer], lru_lambda[layer],
            w_proj_a[layer], w_proj_b[layer], w_proj_c[layer], w_out[layer], norm_mix[layer])
        x2d = _conv_ffn_block(x2d, s, norm_ffn[layer], ffn_w_up[layer], ffn_conv_w[layer],
                              ffn_conv_b[layer], ffn_w_down[layer])
    return x2d.reshape(bsz, s, d)
```

```python
import functools
import math

import numpy as np
import jax
import jax.numpy as jnp
from jax import lax
from jax.experimental import pallas as pl
from jax.experimental.pallas import tpu as pltpu

F32 = jnp.float32
MXU_DTYPE = jnp.bfloat16

LANES = 128
SUBLANES = 8
VMEM_LIMIT_BYTES = 58 * 1024 * 1024

HEAD_DIM = 128
ROPE_THETA = 500000.0
ROT_DIM = HEAD_DIM // 4
NORM_EPS = 1e-6
DIL_GROUPS = ((128, 1), (512, 4), (2048, 16))
A_HEADS_PER_GROUP = 4
A_HEADS = A_HEADS_PER_GROUP * len(DIL_GROUPS)
A_BLOCK = 128
B_HEADS = 12
B_TOPK_MAX = 256
IDX_HEADS = 16
IDX_DIM = 64
IDX_ROT = IDX_DIM // 4
LRU_BLOCKS = 16
LRU_C = 8.0
LRU_CONV_W = 4
FFN_CONV_W = 3
N_BRANCH = 3
A_GROUP_WIDTH = A_HEADS_PER_GROUP * HEAD_DIM

QK_COLS = 5120
V_COLS = 2048
IDX_COLS = 1280
QB_BLOCK = 128
KEY_TILE = 256
INT_MIN = -(2 ** 31)
KEY_NEG_INF = INT_MIN + 0x7FFFFF
HI16_MASK = -65536
BF16_ROWS = 16
COUNT_CHUNK = 1024
BF16_MIN_NORMAL = 0x80
F32_MIN_NORMAL = 1.1754943508222875e-38
NEG_BIG = -1e30
LOG2_E = 1.4426950408889634


def _params(*semantics):
    return pltpu.CompilerParams(dimension_semantics=semantics,
                                vmem_limit_bytes=VMEM_LIMIT_BYTES)


def _gelu_tanh(x):
    return 0.5 * x * (1.0 + jnp.tanh(0.7978845608028654 * (x + 0.044715 * (x * x * x))))


def _sigmoid(x):
    return 1.0 / (1.0 + jnp.exp(-x))


def _rope_tables_kernel(pos_ref, par_ref, c_ref, s1_ref, s2_ref):
    ang = pos_ref[...] * par_ref[0:1, :]
    c = jnp.cos(ang)
    s = jnp.sin(ang)
    c_ref[...] = jnp.where(par_ref[1:2, :] > 0.0, c, 1.0)
    s1_ref[...] = -s * par_ref[2:3, :]
    s2_ref[...] = s * par_ref[3:4, :]


def _rope_tables(pos_col, rot_dim, period):
    t = pos_col.shape[0]
    half = rot_dim // 2
    freq = ROPE_THETA ** (-jnp.arange(half, dtype=F32) / half)
    lane = np.arange(LANES) % period
    in_rot = lane < rot_dim
    freq_lane = jnp.where(jnp.asarray(in_rot), freq[lane % half], 0.0)
    rows = [freq_lane,
            jnp.asarray(in_rot, F32),
            jnp.asarray(lane < half, F32),
            jnp.asarray((lane >= half) & in_rot, F32)]
    par = jnp.stack(rows + [jnp.zeros((LANES,), F32)] * 4, axis=0)
    tm = min(t, 2048)
    spec = pl.BlockSpec((tm, LANES), lambda i: (i, 0))
    return pl.pallas_call(
        _rope_tables_kernel,
        grid=(t // tm,),
        in_specs=[pl.BlockSpec((tm, 1), lambda i: (i, 0)),
                  pl.BlockSpec((SUBLANES, LANES), lambda i: (0, 0))],
        out_specs=[spec, spec, spec],
        out_shape=[jax.ShapeDtypeStruct((t, LANES), F32)] * 3,
        compiler_params=_params("parallel"),
        name="rope_tables",
    )(pos_col, par)


def _norm_proj_kernel(n_rows, n_cols, epilogue, x_ref, g_ref, w_ref, *rest):
    row_refs = rest[:n_rows]
    col_refs = rest[n_rows:n_rows + n_cols]
    o_ref, h_ref = rest[n_rows + n_cols:]

    @pl.when(pl.program_id(1) == 0)
    def _():
        xf = x_ref[...]
        ms = jnp.mean(xf * xf, axis=-1, keepdims=True)
        h_ref[...] = (xf * lax.rsqrt(ms + NORM_EPS) * g_ref[...]).astype(h_ref.dtype)

    acc = jnp.dot(h_ref[...], w_ref[...], preferred_element_type=F32)
    if epilogue is not None:
        acc = epilogue(acc, [r[...] for r in row_refs], [c[...] for c in col_refs])
    o_ref[...] = acc.astype(o_ref.dtype)


def _norm_proj(x2d, gain, w, *, tn, out_dtype, name, epilogue=None, row_inputs=(), col_inputs=()):
    t, d = x2d.shape
    n = w.shape[1]
    tm = min(t, 1024)
    in_specs = [pl.BlockSpec((tm, d), lambda i, j: (i, 0)),
                pl.BlockSpec((1, d), lambda i, j: (0, 0)),
                pl.BlockSpec((d, tn), lambda i, j: (0, j))]
    in_specs += [pl.BlockSpec((tm, LANES), lambda i, j: (i, 0)) for _ in row_inputs]
    in_specs += [pl.BlockSpec((1, tn), lambda i, j: (0, j)) for _ in col_inputs]
    return pl.pallas_call(
        functools.partial(_norm_proj_kernel, len(row_inputs), len(col_inputs), epilogue),
        grid=(t // tm, n // tn),
        in_specs=in_specs,
        out_specs=pl.BlockSpec((tm, tn), lambda i, j: (i, j)),
        out_shape=jax.ShapeDtypeStruct((t, n), out_dtype),
        scratch_shapes=[pltpu.VMEM((tm, d), MXU_DTYPE)],
        compiler_params=_params("parallel", "arbitrary"),
        name=name,
    )(x2d, gain, w, *row_inputs, *col_inputs)


def _rope_lanes(y, tables, half):
    c, s1, s2 = tables
    return (y * c + pltpu.roll(y, LANES - half, 1) * s1 + pltpu.roll(y, half, 1) * s2)


def _ep_headnorm_rope(acc, rows, cols):
    gain, = cols
    outs = []
    for c in range(acc.shape[1] // LANES):
        y = acc[:, c * LANES:(c + 1) * LANES]
        ms = jnp.mean(y * y, axis=-1, keepdims=True)
        y = y * lax.rsqrt(ms + NORM_EPS) * gain[:, c * LANES:(c + 1) * LANES]
        outs.append(_rope_lanes(y, rows, ROT_DIM // 2))
    return jnp.concatenate(outs, axis=1)


def _ep_rope_idx(acc, rows, cols):
    outs = []
    for c in range(acc.shape[1] // LANES):
        outs.append(_rope_lanes(acc[:, c * LANES:(c + 1) * LANES], rows, IDX_ROT // 2))
    return jnp.concatenate(outs, axis=1)


def _dilated_kernel(span, q_ref, kc_ref, kp_ref, vc_ref, vp_ref, o_ref, lse_ref):
    i = pl.program_id(2)
    qi = lax.broadcasted_iota(jnp.int32, (A_BLOCK, 2 * A_BLOCK), 0) + A_BLOCK
    kj = lax.broadcasted_iota(jnp.int32, (A_BLOCK, 2 * A_BLOCK), 1)
    dist = qi - kj
    min_key = jnp.where(i > 0, 0, A_BLOCK)
    mask = (dist >= 0) & (dist <= span) & (kj >= min_key)
    for h in range(A_HEADS_PER_GROUP):
        hs = slice(h * HEAD_DIM, (h + 1) * HEAD_DIM)
        q = q_ref[:, hs]
        k2 = jnp.concatenate([kp_ref[:, hs], kc_ref[:, hs]], axis=0)
        v2 = jnp.concatenate([vp_ref[:, hs], vc_ref[:, hs]], axis=0)
        s = lax.dot_general(q, k2, (((1,), (1,)), ((), ())), preferred_element_type=F32)
        s = jnp.where(mask, s, NEG_BIG)
        m = jnp.max(s, axis=-1, keepdims=True)
        p = jnp.exp(s - m)
        l = jnp.sum(p, axis=-1, keepdims=True)
        o = jnp.dot(p.astype(v2.dtype), v2, preferred_element_type=F32)
        o_ref[:, hs] = o / l
        lse_ref[:, hs] = jnp.broadcast_to(m + jnp.log(l), (A_BLOCK, HEAD_DIM))


def _dilated_group(qk, v, bsz, s, g, window, dil):
    n = s // dil
    nb = n // A_BLOCK
    cq = QK_COLS // A_GROUP_WIDTH
    cv = V_COLS // A_GROUP_WIDTH
    qk3 = qk.reshape(bsz, n, dil * QK_COLS)
    v3 = v.reshape(bsz, n, dil * V_COLS)
    blk = (None, A_BLOCK, A_GROUP_WIDTH)
    k_off = A_HEADS // A_HEADS_PER_GROUP

    def prev(i):
        return jnp.maximum(i - 1, 0)

    in_specs = [
        pl.BlockSpec(blk, lambda b, r, i: (b, i, r * cq + g)),
        pl.BlockSpec(blk, lambda b, r, i: (b, i, r * cq + k_off + g)),
        pl.BlockSpec(blk, lambda b, r, i: (b, prev(i), r * cq + k_off + g)),
        pl.BlockSpec(blk, lambda b, r, i: (b, i, r * cv + g)),
        pl.BlockSpec(blk, lambda b, r, i: (b, prev(i), r * cv + g)),
    ]
    out_spec = pl.BlockSpec(blk, lambda b, r, i: (b, i, r))
    out_sds = jax.ShapeDtypeStruct((bsz, n, dil * A_GROUP_WIDTH), F32)
    o, lse = pl.pallas_call(
        functools.partial(_dilated_kernel, window // dil),
        grid=(bsz, dil, nb),
        in_specs=in_specs,
        out_specs=[out_spec, out_spec],
        out_shape=[out_sds, out_sds],
        compiler_params=_params("parallel", "parallel", "arbitrary"),
        name=f"dilated_attn_g{g}",
    )(qk3, qk3, qk3, v3, v3)
    return o.reshape(bsz * s, A_GROUP_WIDTH), lse.reshape(bsz * s, A_GROUP_WIDTH)


def _merge_groups_kernel(o0, o1, o2, l0, l1, l2, out_ref):
    a, b, c = l0[...], l1[...], l2[...]
    m = jnp.maximum(jnp.maximum(a, b), c)
    ea, eb, ec = jnp.exp(a - m), jnp.exp(b - m), jnp.exp(c - m)
    num = ea * o0[...] + eb * o1[...] + ec * o2[...]
    out_ref[...] = (num / (ea + eb + ec)).astype(out_ref.dtype)


def _merge_groups(os_, lses):
    t = os_[0].shape[0]
    tm = min(t, 1024)
    spec = pl.BlockSpec((tm, A_GROUP_WIDTH), lambda i: (i, 0))
    return pl.pallas_call(
        _merge_groups_kernel,
        grid=(t // tm,),
        in_specs=[spec] * 6,
        out_specs=spec,
        out_shape=jax.ShapeDtypeStruct((t, A_GROUP_WIDTH), MXU_DTYPE),
        compiler_params=_params("parallel"),
        name="dilated_merge",
    )(*os_, *lses)


def _tree_row_sum(x, group):
    parts = [x[g * group:(g + 1) * group, :] for g in range(x.shape[0] // group)]
    while len(parts) > 1:
        parts = [a + b for a, b in zip(parts[0::2], parts[1::2])]
    return parts[0]


def _sparse_kernel(topk, idx_scale, qit_ref, wt_ref, ki_ref, qbt_ref, kb_ref, vt_ref,
                   out_ref, keys_ref, hi_ref, acc_ref, m_ref, s_idx0, s_idx1, s_att0, s_att1,
                   p0, p1, al0, al1, bias_buf):
    qb = pl.program_id(1)
    t0 = qb * QB_BLOCK
    n_pairs = (t0 + QB_BLOCK + 2 * KEY_TILE - 1) // (2 * KEY_TILE)
    n_tiles = 2 * n_pairs
    last_tile = n_tiles - 1
    lane_q = lax.broadcasted_iota(jnp.int32, (KEY_TILE, QB_BLOCK), 1) + t0
    row_id = lax.broadcasted_iota(jnp.int32, (KEY_TILE, QB_BLOCK), 0)
    query_pos = lax.broadcasted_iota(jnp.int32, (1, QB_BLOCK), 1) + t0
    few_keys = query_pos < topk
    wts = wt_ref[...] * idx_scale

    def tile_start(kt):
        return pl.multiple_of(kt * KEY_TILE, KEY_TILE)

    def idx_matmul(kt, buf):
        k_tile = ki_ref[pl.ds(tile_start(kt), KEY_TILE), 0:IDX_DIM]
        buf[...] = jnp.dot(k_tile, qit_ref[...], preferred_element_type=F32)

    def idx_reduce(kt, buf):
        r0 = tile_start(kt)
        acc = jnp.zeros((KEY_TILE, QB_BLOCK), F32)
        for h in range(IDX_HEADS):
            acc = acc + jnp.maximum(buf[:, h * QB_BLOCK:(h + 1) * QB_BLOCK], 0.0) * wts[h:h + 1, :]
        acc = jnp.where(jnp.abs(acc) < F32_MIN_NORMAL, 0.0, acc)
        acc = jnp.where(row_id + r0 <= lane_q, acc, -jnp.inf)
        bits = pltpu.bitcast(acc, jnp.int32)
        keys_ref[pl.ds(r0, KEY_TILE), :] = bits ^ ((bits >> 31) & 0x7FFFFFFF)
        hi_ref[pl.ds(r0, KEY_TILE), :] = pltpu.bitcast(bits & HI16_MASK, F32).astype(jnp.bfloat16)

    idx_matmul(0, s_idx0)

    def score_pair(i, carry):
        idx_matmul(2 * i + 1, s_idx1)
        idx_reduce(2 * i, s_idx0)
        idx_matmul(jnp.minimum(2 * i + 2, last_tile), s_idx0)
        idx_reduce(2 * i + 1, s_idx1)
        return carry

    lax.fori_loop(0, n_pairs, score_pair, 0)

    def sum_over_keys(partial, acc_rows, dtype):
        per_chunk = COUNT_CHUNK // KEY_TILE
        n_chunks = n_tiles // per_chunk

        def chunk(c, acc):
            return acc + partial(pl.multiple_of(c * COUNT_CHUNK, COUNT_CHUNK), COUNT_CHUNK)

        def tile(kt, acc):
            return acc + partial(tile_start(kt), KEY_TILE)

        acc = lax.fori_loop(0, n_chunks, chunk, jnp.zeros((acc_rows, QB_BLOCK), dtype))
        acc = lax.fori_loop(n_chunks * per_chunk, n_tiles, tile, acc)
        return jnp.sum(acc, axis=0, keepdims=True)

    def count_prefix_ge(prefix):
        prefix = jnp.where((prefix > 0) & (prefix < BF16_MIN_NORMAL), BF16_MIN_NORMAL, prefix)
        fbits = (prefix ^ ((prefix >> 15) & 0x7FFF)) << 16
        cand = pltpu.bitcast(fbits, F32).astype(jnp.bfloat16)

        def partial(r0, rows):
            hit = jnp.where(hi_ref[pl.ds(r0, rows), :] >= cand,
                            jnp.ones((rows, QB_BLOCK), jnp.bfloat16),
                            jnp.zeros((rows, QB_BLOCK), jnp.bfloat16))
            return _tree_row_sum(hit, BF16_ROWS).astype(F32)

        return sum_over_keys(partial, BF16_ROWS, F32)

    want = jnp.float32(topk)
    zero_prefix = jnp.zeros((1, QB_BLOCK), jnp.int32)
    cnt0 = count_prefix_ge(zero_prefix)
    take0 = cnt0 >= want
    prefix = jnp.where(take0, zero_prefix, -(2 ** 15))
    cnt = jnp.where(take0, cnt0, jnp.float32(2 ** 24))

    def prefix_step(i, carry):
        prefix, cnt = carry
        cand = prefix | lax.shift_left(jnp.int32(1), 14 - i)
        c = count_prefix_ge(cand)
        take = c >= want
        return jnp.where(take, cand, prefix), jnp.where(take, c, cnt)

    prefix, cnt = lax.fori_loop(0, 15, prefix_step, (prefix, cnt))

    def count_where(pred):
        def partial(r0, rows):
            rid = lax.broadcasted_iota(jnp.int32, (rows, QB_BLOCK), 0) + r0
            hit = jnp.where(pred(keys_ref[pl.ds(r0, rows), :], rid), 1, 0)
            return _tree_row_sum(hit, SUBLANES)
        return sum_over_keys(partial, SUBLANES, jnp.int32)

    def count_ge(cand):
        return count_where(lambda k, rid: k >= cand)

    def unresolved(cnt):
        return jnp.max(jnp.where(few_keys | (cnt == topk), 0, 1)) > 0

    def low_cond(carry):
        bit, _, cnt = carry
        return (bit >= 0) & unresolved(cnt)

    def low_step(carry):
        bit, tau, cnt = carry
        cand = tau | lax.shift_left(jnp.int32(1), bit)
        c = count_ge(cand)
        take = c >= topk
        return bit - 1, jnp.where(take, cand, tau), jnp.where(take, c, cnt)

    _, tau, n_ge = lax.while_loop(low_cond, low_step,
                                  (jnp.int32(15), prefix << 16, cnt.astype(jnp.int32)))
    tau = jnp.where(few_keys, KEY_NEG_INF + 1, tau)

    @pl.when(jnp.max(jnp.where(few_keys, 0, n_ge)) > topk)
    def _():
        need = topk - count_ge(tau + 1)

        def idx_step(i, last):
            cand = last | lax.shift_left(jnp.int32(1), 30 - i)
            below = count_where(lambda k, rid: (k == tau) & (rid < cand))
            return jnp.where(below < need, cand, last)

        last = lax.fori_loop(0, 31, idx_step, jnp.zeros((1, QB_BLOCK), jnp.int32))

        def demote(kt, carry):
            r0 = tile_start(kt)
            k = keys_ref[pl.ds(r0, KEY_TILE), :]
            drop = (k == tau) & (row_id + r0 > last)
            keys_ref[pl.ds(r0, KEY_TILE), :] = jnp.where(drop, tau - 1, k)
            return carry

        lax.fori_loop(0, n_tiles, demote, 0)

    def att_matmul(kt, s_buf):
        s_buf[...] = jnp.dot(kb_ref[pl.ds(tile_start(kt), KEY_TILE), :], qbt_ref[...],
                             preferred_element_type=F32)

    def softmax_tile(kt, s_buf, p_buf, al_buf):
        sel = keys_ref[pl.ds(tile_start(kt), KEY_TILE), :] >= tau
        bias_buf[...] = jnp.where(sel, 0.0, NEG_BIG)
        for h in range(B_HEADS):
            hs = slice(h * QB_BLOCK, (h + 1) * QB_BLOCK)
            s = s_buf[:, hs] + bias_buf[...]
            m_old = m_ref[:, hs]
            m_new = jnp.maximum(m_old, jnp.max(s, axis=0, keepdims=True))
            m_ref[:, hs] = m_new
            al_buf[:, hs] = jnp.exp2(m_old - m_new)
            p_buf[:, hs] = jnp.exp2(s - m_new).astype(p_buf.dtype)

    def value_matmul(kt, p_buf, al_buf):
        pv = jnp.dot(vt_ref[:, pl.ds(tile_start(kt), KEY_TILE)], p_buf[...],
                     preferred_element_type=F32)
        acc_ref[...] = acc_ref[...] * al_buf[...] + pv

    m_ref[...] = jnp.full(m_ref.shape, NEG_BIG, F32)
    acc_ref[...] = jnp.zeros(acc_ref.shape, F32)
    p1[...] = jnp.zeros(p1.shape, p1.dtype)
    al1[...] = jnp.ones(al1.shape, F32)
    att_matmul(0, s_att0)

    def attend_pair(i, carry):
        att_matmul(2 * i + 1, s_att1)
        value_matmul(jnp.maximum(2 * i - 1, 0), p1, al1)
        softmax_tile(2 * i, s_att0, p0, al0)
        att_matmul(jnp.minimum(2 * i + 2, last_tile), s_att0)
        value_matmul(2 * i, p0, al0)
        softmax_tile(2 * i + 1, s_att1, p1, al1)
        return carry

    lax.fori_loop(0, n_pairs, attend_pair, 0)
    value_matmul(last_tile, p1, al1)

    for h in range(B_HEADS):
        hs = slice(h * QB_BLOCK, (h + 1) * QB_BLOCK)
        o_t = acc_ref[0:HEAD_DIM, hs] / acc_ref[HEAD_DIM:HEAD_DIM + 1, hs]
        out_ref[:, h * HEAD_DIM:(h + 1) * HEAD_DIM] = o_t.T.astype(out_ref.dtype)


def _sparse_mixer(qk, v, idx, wi_t, bsz, s):
    nqb = s // QB_BLOCK
    topk = min(B_TOPK_MAX, s // 4)
    idx_scale = (IDX_HEADS ** -0.5) * (IDX_DIM ** -0.5)
    nq = B_HEADS * QB_BLOCK
    qi = idx[:, :IDX_HEADS * IDX_DIM].reshape(bsz, nqb, QB_BLOCK, IDX_HEADS, IDX_DIM)
    qit = qi.transpose(0, 1, 4, 3, 2).reshape(bsz, nqb, IDX_DIM, IDX_HEADS * QB_BLOCK)
    qb_off = 2 * A_HEADS * HEAD_DIM
    qbq = qk[:, qb_off:qb_off + B_HEADS * HEAD_DIM].reshape(bsz, nqb, QB_BLOCK, B_HEADS, HEAD_DIM)
    qbt = qbq.transpose(0, 1, 4, 3, 2).reshape(bsz, nqb, HEAD_DIM, nq)
    vb = v[:, A_HEADS * HEAD_DIM:A_HEADS * HEAD_DIM + HEAD_DIM].reshape(bsz, s, HEAD_DIM)
    vt = jnp.concatenate([vb.transpose(0, 2, 1),
                          jnp.ones((bsz, 1, s), vb.dtype),
                          jnp.zeros((bsz, SUBLANES - 1, s), vb.dtype)], axis=1)
    v_rows = HEAD_DIM + SUBLANES
    qk3 = qk.reshape(bsz, s, QK_COLS)
    idx3 = idx.reshape(bsz, s, IDX_COLS)
    kb_blk = (qb_off + B_HEADS * HEAD_DIM) // LANES
    ki_blk = (IDX_HEADS * IDX_DIM) // LANES
    out = pl.pallas_call(
        functools.partial(_sparse_kernel, topk, idx_scale),
        grid=(bsz, nqb),
        in_specs=[
            pl.BlockSpec((None, None, IDX_DIM, IDX_HEADS * QB_BLOCK), lambda b, q: (b, q, 0, 0)),
            pl.BlockSpec((None, None, IDX_HEADS, QB_BLOCK), lambda b, q: (b, q, 0, 0)),
            pl.BlockSpec((None, s, LANES), lambda b, q: (b, 0, ki_blk)),
            pl.BlockSpec((None, None, HEAD_DIM, nq), lambda b, q: (b, q, 0, 0)),
            pl.BlockSpec((None, s, LANES), lambda b, q: (b, 0, kb_blk)),
            pl.BlockSpec((None, v_rows, s), lambda b, q: (b, 0, 0)),
        ],
        out_specs=pl.BlockSpec((None, QB_BLOCK, B_HEADS * HEAD_DIM), lambda b, q: (b, q, 0)),
        out_shape=jax.ShapeDtypeStruct((bsz, s, B_HEADS * HEAD_DIM), MXU_DTYPE),
        scratch_shapes=[pltpu.VMEM((s, QB_BLOCK), jnp.int32),
                        pltpu.VMEM((s, QB_BLOCK), jnp.bfloat16),
                        pltpu.VMEM((v_rows, nq), F32),
                        pltpu.VMEM((1, nq), F32),
                        pltpu.VMEM((KEY_TILE, IDX_HEADS * QB_BLOCK), F32),
                        pltpu.VMEM((KEY_TILE, IDX_HEADS * QB_BLOCK), F32),
                        pltpu.VMEM((KEY_TILE, nq), F32),
                        pltpu.VMEM((KEY_TILE, nq), F32),
                        pltpu.VMEM((KEY_TILE, nq), MXU_DTYPE),
                        pltpu.VMEM((KEY_TILE, nq), MXU_DTYPE),
                        pltpu.VMEM((1, nq), F32),
                        pltpu.VMEM((1, nq), F32),
                        pltpu.VMEM((KEY_TILE, QB_BLOCK), F32)],
        compiler_params=_params("parallel", "arbitrary"),
        name="sparse_mixer",
    )(qit, wi_t, idx3, qbt, qk3, vt)
    return out.reshape(bsz * s, B_HEADS * HEAD_DIM)


def _lru_kernel(ts, xc_ref, yc_ref, cw_ref, cb_ref, wa_ref, wx_ref, ba_ref, bx_ref, lam_ref,
                o_ref, xbuf, a_buf, b_buf, h_state):
    halo = SUBLANES

    @pl.when(pl.program_id(1) == 0)
    def _():
        xbuf[0:halo, :] = jnp.zeros((halo, xbuf.shape[1]), F32)
        h_state[...] = jnp.zeros(h_state.shape, F32)

    xbuf[halo:halo + ts, :] = xc_ref[...]
    xc = cb_ref[...] + cw_ref[0:1, :] * xbuf[halo - 3:halo - 3 + ts, :]
    for tap in range(1, LRU_CONV_W):
        lo = halo - (LRU_CONV_W - 1) + tap
        xc = xc + cw_ref[tap:tap + 1, :] * xbuf[lo:lo + ts, :]
    xbuf[0:halo, :] = xc_ref[ts - halo:ts, :]

    z = -lam_ref[...]
    softplus = jnp.maximum(z, 0.0) + jnp.log1p(jnp.exp(-jnp.abs(z)))
    bw = xc.shape[1] // LRU_BLOCKS
    for blk in range(LRU_BLOCKS):
        cs = slice(blk * bw, (blk + 1) * bw)
        xs = xc[:, cs]
        xs_m = xs.astype(MXU_DTYPE)
        r = _sigmoid(jnp.dot(xs_m, wa_ref[blk], preferred_element_type=F32) + ba_ref[:, cs])
        gate_i = _sigmoid(jnp.dot(xs_m, wx_ref[blk], preferred_element_type=F32) + bx_ref[:, cs])
        log_a = (-LRU_C) * r * softplus[:, cs]
        a = jnp.exp(log_a)
        a_buf[:, cs] = a
        b_buf[:, cs] = jnp.sqrt(-jnp.tanh(log_a) * (a * a + 1.0)) * gate_i * xs

    def step(t, h):
        h = a_buf[pl.ds(t, 1), :] * h + b_buf[pl.ds(t, 1), :]
        b_buf[pl.ds(t, 1), :] = h
        return h

    h_state[0:1, :] = lax.fori_loop(0, ts, step, h_state[0:1, :], unroll=8)
    o_ref[...] = (b_buf[...] * _gelu_tanh(yc_ref[...])).astype(o_ref.dtype)


def _lru_mixer(feat, bsz, s, width, conv_w, conv_b, wa, ba, wx, bx, lam):
    ts = min(s, 512)
    nt = s // ts
    row = lambda a: a.reshape(1, width).astype(F32)
    vec = pl.BlockSpec((1, width), lambda b, t: (0, 0))
    wspec = pl.BlockSpec(wa.shape, lambda b, t: (0, 0, 0))
    return pl.pallas_call(
        functools.partial(_lru_kernel, ts),
        grid=(bsz, nt),
        in_specs=[pl.BlockSpec((ts, width), lambda b, t: (b * nt + t, 0)),
                  pl.BlockSpec((ts, width), lambda b, t: (b * nt + t, 1)),
                  pl.BlockSpec((LRU_CONV_W, width), lambda b, t: (0, 0)),
                  vec, wspec, wspec, vec, vec, vec],
        out_specs=pl.BlockSpec((ts, width), lambda b, t: (b * nt + t, 0)),
        out_shape=jax.ShapeDtypeStruct((bsz * s, width), MXU_DTYPE),
        scratch_shapes=[pltpu.VMEM((ts + SUBLANES, width), F32),
                        pltpu.VMEM((ts, width), F32),
                        pltpu.VMEM((ts, width), F32),
                        pltpu.VMEM((SUBLANES, width), F32)],
        compiler_params=_params("parallel", "arbitrary"),
        name="rglru_mixer",
    )(feat, feat, conv_w.astype(F32), row(conv_b), wa.astype(MXU_DTYPE), wx.astype(MXU_DTYPE),
      row(ba), row(bx), row(lam))


def _gate_merge_kernel(oa, ob, oc, wa, wb, wc, g0, g1, g2, b0, b1, b2, out_ref):
    acc = _sigmoid(g0[...] + b0[...]) * jnp.dot(oa[...], wa[...], preferred_element_type=F32)
    acc = acc + _sigmoid(g1[...] + b1[...]) * jnp.dot(ob[...], wb[...], preferred_element_type=F32)
    acc = acc + _sigmoid(g2[...] + b2[...]) * jnp.dot(oc[...], wc[...], preferred_element_type=F32)
    out_ref[...] = acc.astype(out_ref.dtype)


def _gate_merge(out_a, out_b, out_c, wpa, wpb, wpc, feat, gate_col0, b_gate_row, d_model):
    t = out_a.shape[0]
    tm = min(t, 1024)
    tn = 512
    nj = d_model // tn
    g_blk = gate_col0 // tn

    def act(a):
        return pl.BlockSpec((tm, a.shape[1]), lambda i, j: (i, 0))

    def wgt(w):
        return pl.BlockSpec((w.shape[0], tn), lambda i, j: (0, j))

    gates = [pl.BlockSpec((tm, tn), functools.partial(lambda br, i, j: (i, g_blk + br * nj + j), br))
             for br in range(N_BRANCH)]
    biases = [pl.BlockSpec((1, tn), functools.partial(lambda br, i, j: (0, br * nj + j), br))
              for br in range(N_BRANCH)]
    return pl.pallas_call(
        _gate_merge_kernel,
        grid=(t // tm, nj),
        in_specs=[act(out_a), act(out_b), act(out_c), wgt(wpa), wgt(wpb), wgt(wpc)] + gates + biases,
        out_specs=pl.BlockSpec((tm, tn), lambda i, j: (i, j)),
        out_shape=jax.ShapeDtypeStruct((t, d_model), MXU_DTYPE),
        compiler_params=_params("parallel", "arbitrary"),
        name="gate_merge",
    )(out_a, out_b, out_c, wpa, wpb, wpc, feat, feat, feat, b_gate_row, b_gate_row, b_gate_row)


def _matmul_residual_kernel(a_ref, w_ref, r_ref, o_ref):
    o_ref[...] = r_ref[...] + jnp.dot(a_ref[...], w_ref[...], preferred_element_type=F32)


def _matmul_residual(a, w, resid, name):
    t, k = a.shape
    n = w.shape[1]
    tm = min(t, 1024)
    tn = 512
    return pl.pallas_call(
        _matmul_residual_kernel,
        grid=(t // tm, n // tn),
        in_specs=[pl.BlockSpec((tm, k), lambda i, j: (i, 0)),
                  pl.BlockSpec((k, tn), lambda i, j: (0, j)),
                  pl.BlockSpec((tm, tn), lambda i, j: (i, j))],
        out_specs=pl.BlockSpec((tm, tn), lambda i, j: (i, j)),
        out_shape=jax.ShapeDtypeStruct((t, n), F32),
        compiler_params=_params("parallel", "arbitrary"),
        name=name,
    )(a, w, resid)


def _ffn_up_kernel(tm, tiles_per_seq, x_ref, xh_ref, g_ref, wa_ref, wb_ref, cwa_ref, cwb_ref,
                   cba_ref, cbb_ref, o_ref, h_ref, hh_ref, ua_buf, ub_buf):
    halo = SUBLANES

    def normed(ref):
        xf = ref[...]
        ms = jnp.mean(xf * xf, axis=-1, keepdims=True)
        return (xf * lax.rsqrt(ms + NORM_EPS) * g_ref[...]).astype(MXU_DTYPE)

    @pl.when(pl.program_id(1) == 0)
    def _():
        h_ref[...] = normed(x_ref)
        hh_ref[...] = normed(xh_ref)

    keep = jnp.where(pl.program_id(0) % tiles_per_seq == 0, 0.0, 1.0)

    def conv_half(w_ref, cw_ref, cb_ref, buf):
        buf[0:halo, :] = keep * jnp.dot(hh_ref[...], w_ref[...], preferred_element_type=F32)
        buf[halo:halo + tm, :] = jnp.dot(h_ref[...], w_ref[...], preferred_element_type=F32)
        y = cb_ref[...] + cw_ref[0:1, :] * buf[halo - 2:halo - 2 + tm, :]
        for tap in range(1, FFN_CONV_W):
            lo = halo - (FFN_CONV_W - 1) + tap
            y = y + cw_ref[tap:tap + 1, :] * buf[lo:lo + tm, :]
        return y

    ya = conv_half(wa_ref, cwa_ref, cba_ref, ua_buf)
    yb = conv_half(wb_ref, cwb_ref, cbb_ref, ub_buf)
    o_ref[...] = (_gelu_tanh(ya) * yb).astype(o_ref.dtype)


def _ffn_up(x2d, s, gain, w_up, conv_w, conv_b_row):
    t, d = x2d.shape
    d_ff = w_up.shape[1] // 2
    tm = min(s, 1024)
    tn = 512
    nj = d_ff // tn
    hb = tm // SUBLANES
    return pl.pallas_call(
        functools.partial(_ffn_up_kernel, tm, s // tm),
        grid=(t // tm, nj),
        in_specs=[pl.BlockSpec((tm, d), lambda i, j: (i, 0)),
                  pl.BlockSpec((SUBLANES, d), lambda i, j: (jnp.maximum(i * hb - 1, 0), 0)),
                  pl.BlockSpec((1, d), lambda i, j: (0, 0)),
                  pl.BlockSpec((d, tn), lambda i, j: (0, j)),
                  pl.BlockSpec((d, tn), lambda i, j: (0, nj + j)),
                  pl.BlockSpec((FFN_CONV_W, tn), lambda i, j: (0, j)),
                  pl.BlockSpec((FFN_CONV_W, tn), lambda i, j: (0, nj + j)),
                  pl.BlockSpec((1, tn), lambda i, j: (0, j)),
                  pl.BlockSpec((1, tn), lambda i, j: (0, nj + j))],
        out_specs=pl.BlockSpec((tm, tn), lambda i, j: (i, j)),
        out_shape=jax.ShapeDtypeStruct((t, d_ff), MXU_DTYPE),
        scratch_shapes=[pltpu.VMEM((tm, d), MXU_DTYPE),
                        pltpu.VMEM((SUBLANES, d), MXU_DTYPE),
                        pltpu.VMEM((tm + SUBLANES, tn), F32),
                        pltpu.VMEM((tm + SUBLANES, tn), F32)],
        compiler_params=_params("parallel", "arbitrary"),
        name="ffn_up_conv_gate",
    )(x2d, x2d, gain, w_up, w_up, conv_w, conv_w, conv_b_row, conv_b_row)


def _pad_cols(parts, total, dtype):
    width = sum(p.shape[1] for p in parts)
    rows = parts[0].shape[0]
    if total > width:
        parts = list(parts) + [jnp.zeros((rows, total - width), parts[0].dtype)]
    return jnp.concatenate(parts, axis=1).astype(dtype)


def _temporal_block(x2d, bsz, s, tabs_head, tabs_idx, w_in, b_gate, qn_a, kn_a, qn_b, kn_b,
                    lru_conv_w, lru_conv_b, lru_wa, lru_ba, lru_wx, lru_bx, lru_lambda,
                    w_proj_a, w_proj_b, w_proj_c, w_out, norm_mix):
    d = x2d.shape[1]
    a_w = A_HEADS * HEAD_DIM
    b_w = B_HEADS * HEAD_DIM
    sizes = (a_w, a_w, a_w, b_w, HEAD_DIM, HEAD_DIM, IDX_HEADS * IDX_DIM, IDX_DIM, IDX_HEADS,
             d, d, N_BRANCH * d)
    offs = np.concatenate([[0], np.cumsum(sizes)])
    col = lambda k: w_in[:, offs[k]:offs[k + 1]]
    w_qa, w_ka, w_va, w_qb, w_kb, w_vb, w_qi, w_ki, w_wi, w_xc, w_yc, w_g = [col(k) for k in range(12)]

    gain_row = norm_mix.reshape(1, d).astype(F32)
    att_scale = HEAD_DIM ** -0.5
    head_gain = jnp.concatenate([
        jnp.tile(qn_a.astype(F32) * att_scale, A_HEADS), jnp.tile(kn_a.astype(F32), A_HEADS),
        jnp.tile(qn_b.astype(F32) * (att_scale * LOG2_E), B_HEADS), kn_b.astype(F32),
        jnp.ones((QK_COLS - 2 * a_w - b_w - HEAD_DIM,), F32)]).reshape(1, QK_COLS)

    qk = _norm_proj(x2d, gain_row, _pad_cols([w_qa, w_ka, w_qb, w_kb], QK_COLS, MXU_DTYPE),
                    tn=512, out_dtype=MXU_DTYPE, name="proj_qk", epilogue=_ep_headnorm_rope,
                    row_inputs=tabs_head, col_inputs=(head_gain,))
    v = _norm_proj(x2d, gain_row, _pad_cols([w_va, w_vb], V_COLS, MXU_DTYPE),
                   tn=512, out_dtype=MXU_DTYPE, name="proj_v")
    idx = _norm_proj(x2d, gain_row, _pad_cols([w_qi, w_ki], IDX_COLS, MXU_DTYPE),
                     tn=256, out_dtype=MXU_DTYPE, name="proj_idx", epilogue=_ep_rope_idx,
                     row_inputs=tabs_idx)
    gate_col0 = 2 * d
    wi_col0 = gate_col0 + N_BRANCH * d
    feat_cols = -(-(wi_col0 + IDX_HEADS) // 512) * 512
    feat = _norm_proj(x2d, gain_row, _pad_cols([w_xc, w_yc, w_g, w_wi], feat_cols, MXU_DTYPE),
                      tn=512, out_dtype=F32, name="proj_feat")

    os_, lses = [], []
    for g, (window, dil) in enumerate(DIL_GROUPS):
        o, lse = _dilated_group(qk, v, bsz, s, g, window, dil)
        os_.append(o)
        lses.append(lse)
    out_a = _merge_groups(os_, lses)

    wi = feat[:, wi_col0:wi_col0 + IDX_HEADS].reshape(bsz, s // QB_BLOCK, QB_BLOCK, IDX_HEADS)
    out_b = _sparse_mixer(qk, v, idx, wi.transpose(0, 1, 3, 2), bsz, s)

    out_c = _lru_mixer(feat, bsz, s, d, lru_conv_w, lru_conv_b, lru_wa, lru_ba, lru_wx, lru_bx,
                       lru_lambda)

    merged = _gate_merge(out_a, out_b, out_c, w_proj_a.astype(MXU_DTYPE),
                         w_proj_b.astype(MXU_DTYPE), w_proj_c.astype(MXU_DTYPE), feat, gate_col0,
                         b_gate.reshape(1, N_BRANCH * d).astype(F32), d)
    return _matmul_residual(merged, w_out.astype(MXU_DTYPE), x2d, "out_proj_residual")


def _conv_ffn_block(x2d, s, norm_ffn, ffn_w_up, ffn_conv_w, ffn_conv_b, ffn_w_down):
    d = x2d.shape[1]
    act = _ffn_up(x2d, s, norm_ffn.reshape(1, d).astype(F32), ffn_w_up.astype(MXU_DTYPE),
                  ffn_conv_w.astype(F32), ffn_conv_b.reshape(1, -1).astype(F32))
    return _matmul_residual(act, ffn_w_down.astype(MXU_DTYPE), x2d, "ffn_down_residual")


def kernel(x, positions, w_in, b_gate, qn_a, kn_a, qn_b, kn_b, lru_conv_w, lru_conv_b, lru_wa,
           lru_ba, lru_wx, lru_bx, lru_lambda, w_proj_a, w_proj_b, w_proj_c, w_out, norm_mix,
           norm_ffn, ffn_w_up, ffn_conv_w, ffn_conv_b, ffn_w_down):
    bsz, s, d = x.shape
    assert s % (DIL_GROUPS[-1][1] * A_BLOCK) == 0 and s % (2 * KEY_TILE) == 0
    x2d = x.reshape(bsz * s, d)
    pos_col = positions.astype(F32).reshape(bsz * s, 1)
    tabs_head = _rope_tables(pos_col, ROT_DIM, HEAD_DIM)
    tabs_idx = _rope_tables(pos_col, IDX_ROT, IDX_DIM)
    for layer in range(w_in.shape[0]):
        x2d = _temporal_block(
            x2d, bsz, s, tabs_head, tabs_idx, w_in[layer], b_gate[layer], qn_a[layer],
            kn_a[layer], qn_b[layer], kn_b[layer], lru_conv_w[layer], lru_conv_b[layer],
            lru_wa[layer], lru_ba[layer], lru_wx[layer], lru_bx[layer], lru_lambda[layer],
            w_proj_a[layer], w_proj_b[layer], w_proj_c[layer], w_out[layer], norm_mix[layer])
        x2d = _conv_ffn_block(x2d, s, norm_ffn[layer], ffn_w_up[layer], ffn_conv_w[layer],
                              ffn_conv_b[layer], ffn_w_down[layer])
    return x2d.reshape(bsz, s, d)
```

```python
import functools
import math

import numpy as np
import jax
import jax.numpy as jnp
from jax import lax
from jax.experimental import pallas as pl
from jax.experimental.pallas import tpu as pltpu

F32 = jnp.float32
MXU_DTYPE = jnp.bfloat16

LANES = 128
SUBLANES = 8
VMEM_LIMIT_BYTES = 58 * 1024 * 1024

HEAD_DIM = 128
ROPE_THETA = 500000.0
ROT_DIM = HEAD_DIM // 4
NORM_EPS = 1e-6
DIL_GROUPS = ((128, 1), (512, 4), (2048, 16))
A_HEADS_PER_GROUP = 4
A_HEADS = A_HEADS_PER_GROUP * len(DIL_GROUPS)
A_BLOCK = 128
B_HEADS = 12
B_TOPK_MAX = 256
IDX_HEADS = 16
IDX_DIM = 64
IDX_ROT = IDX_DIM // 4
LRU_BLOCKS = 16
LRU_C = 8.0
LRU_CONV_W = 4
FFN_CONV_W = 3
N_BRANCH = 3
A_GROUP_WIDTH = A_HEADS_PER_GROUP * HEAD_DIM

B_COLS = 2048
KB_COL = 1536
VB_COL = 1792
IDX_COLS = 1280
EP_CHUNK = 256
PROJ_GROUP = 512
QB_BLOCK = 128
KEY_TILE = 256
INT_MIN = -(2 ** 31)
KEY_NEG_INF = INT_MIN + 0x7FFFFF
HI16_MASK = -65536
BF16_ROWS = 16
COUNT_CHUNK = 1024
BF16_MIN_NORMAL = 0x80
F32_MIN_NORMAL = 1.1754943508222875e-38
NEG_BIG = -1e30
LOG2_E = 1.4426950408889634


def _params(*semantics):
    return pltpu.CompilerParams(dimension_semantics=semantics,
                                vmem_limit_bytes=VMEM_LIMIT_BYTES)


def _gelu_tanh(x):
    return 0.5 * x * (1.0 + jnp.tanh(0.7978845608028654 * (x + 0.044715 * (x * x * x))))


def _sigmoid(x):
    return 1.0 / (1.0 + jnp.exp(-x))


def _rope_tables_kernel(pos_ref, par_ref, c_ref, s_ref):
    ang = pos_ref[...] * par_ref[0:1, :]
    c_ref[...] = jnp.where(par_ref[1:2, :] > 0.0, jnp.cos(ang), 1.0)
    s_ref[...] = jnp.sin(ang) * par_ref[2:3, :]


def _rope_tables(pos_col, rot_dim, period):
    t = pos_col.shape[0]
    half = rot_dim // 2
    freq = ROPE_THETA ** (-jnp.arange(half, dtype=F32) / half)
    lane = np.arange(LANES) % period
    in_rot = lane < rot_dim
    freq_lane = jnp.where(jnp.asarray(in_rot), freq[lane % half], 0.0)
    sign = np.where(lane < half, -1.0, np.where(in_rot, 1.0, 0.0))
    rows = [freq_lane, jnp.asarray(in_rot, F32), jnp.asarray(sign, F32)]
    par = jnp.stack(rows + [jnp.zeros((LANES,), F32)] * 5, axis=0)
    tm = min(t, 2048)
    spec = pl.BlockSpec((tm, LANES), lambda i: (i, 0))
    return pl.pallas_call(
        _rope_tables_kernel,
        grid=(t // tm,),
        in_specs=[pl.BlockSpec((tm, 1), lambda i: (i, 0)),
                  pl.BlockSpec((SUBLANES, LANES), lambda i: (0, 0))],
        out_specs=[spec, spec],
        out_shape=[jax.ShapeDtypeStruct((t, LANES), F32)] * 2,
        compiler_params=_params("parallel"),
        name="rope_tables",
    )(pos_col, par)


def _partner_matrix(head, half):
    m = np.zeros((EP_CHUNK, EP_CHUNK), np.float32)
    for base in range(0, EP_CHUNK, head):
        for i in range(half):
            m[base + i + half, base + i] = 1.0
            m[base + i, base + i + half] = 1.0
    return jnp.asarray(m, MXU_DTYPE)


def _normed(x_ref, g_ref):
    xf = x_ref[...]
    ms = jnp.mean(xf * xf, axis=-1, keepdims=True)
    return (xf * lax.rsqrt(ms + NORM_EPS) * g_ref[...]).astype(MXU_DTYPE)


def _norm_proj_kernel(x_ref, g_ref, w_ref, o_ref, h_ref):
    @pl.when(pl.program_id(1) == 0)
    def _():
        h_ref[...] = _normed(x_ref, g_ref)

    o_ref[...] = jnp.dot(h_ref[...], w_ref[...], preferred_element_type=F32).astype(o_ref.dtype)


def _norm_proj(x2d, gain, w, *, tn, out_dtype, name):
    t, d = x2d.shape
    n = w.shape[1]
    tm = min(t, 1024)
    return pl.pallas_call(
        _norm_proj_kernel,
        grid=(t // tm, n // tn),
        in_specs=[pl.BlockSpec((tm, d), lambda i, j: (i, 0)),
                  pl.BlockSpec((1, d), lambda i, j: (0, 0)),
                  pl.BlockSpec((d, tn), lambda i, j: (0, j))],
        out_specs=pl.BlockSpec((tm, tn), lambda i, j: (i, j)),
        out_shape=jax.ShapeDtypeStruct((t, n), out_dtype),
        scratch_shapes=[pltpu.VMEM((tm, d), MXU_DTYPE)],
        compiler_params=_params("parallel", "arbitrary"),
        name=name,
    )(x2d, gain, w)


NORM_ROPE, ROPE, PLAIN = "norm_rope", "rope", "plain"


def _norm_proj_heads_kernel(dil, kinds, x_ref, g_ref, w_ref, part_ref, c_ref, s_ref, hg_ref,
                            o_ref, y_buf):
    h = _normed(x_ref, g_ref)
    per_chunk = EP_CHUNK // LANES
    cos2 = jnp.concatenate([c_ref[...]] * per_chunk, axis=1)
    sin2 = jnp.concatenate([s_ref[...]] * per_chunk, axis=1)
    n = w_ref.shape[1]
    for g0 in range(0, n, PROJ_GROUP):
        width = min(PROJ_GROUP, n - g0)
        yg = jnp.dot(h, w_ref[:, g0:g0 + width], preferred_element_type=F32)
        for c0 in range(0, width, EP_CHUNK):
            kind = kinds[(g0 + c0) // EP_CHUNK]
            y = yg[:, c0:c0 + EP_CHUNK]
            if kind == NORM_ROPE:
                heads = []
                for k in range(EP_CHUNK // HEAD_DIM):
                    yh = y[:, k * HEAD_DIM:(k + 1) * HEAD_DIM]
                    ms = jnp.mean(yh * yh, axis=-1, keepdims=True)
                    lo = g0 + c0 + k * HEAD_DIM
                    heads.append(yh * lax.rsqrt(ms + NORM_EPS) * hg_ref[:, lo:lo + HEAD_DIM])
                y = jnp.concatenate(heads, axis=1)
            if kind != PLAIN:
                partner = jnp.dot(y.astype(MXU_DTYPE), part_ref[...], preferred_element_type=F32)
                y = y * cos2 + partner * sin2
            for k in range(per_chunk):
                y_buf[(g0 + c0) // LANES + k] = y[:, k * LANES:(k + 1) * LANES]

    rows = y_buf.shape[1] // dil
    for r in range(dil):
        for k in range(y_buf.shape[0]):
            slab = y_buf[k] if dil == 1 else y_buf[k, pl.ds(r, rows, stride=dil), :]
            o_ref[r, :, k * LANES:(k + 1) * LANES] = slab.astype(o_ref.dtype)


def _norm_proj_heads(x2d, gain, w, partner, tabs, head_gain, kinds, *, dil, name):
    t, d = x2d.shape
    n = w.shape[1]
    tm = min(t, 512)
    assert len(kinds) == n // EP_CHUNK
    tab = pl.BlockSpec((tm, LANES), lambda i: (i, 0))
    return pl.pallas_call(
        functools.partial(_norm_proj_heads_kernel, dil, tuple(kinds)),
        grid=(t // tm,),
        in_specs=[pl.BlockSpec((tm, d), lambda i: (i, 0)),
                  pl.BlockSpec((1, d), lambda i: (0, 0)),
                  pl.BlockSpec((d, n), lambda i: (0, 0)),
                  pl.BlockSpec((EP_CHUNK, EP_CHUNK), lambda i: (0, 0)),
                  tab, tab,
                  pl.BlockSpec((1, n), lambda i: (0, 0))],
        out_specs=pl.BlockSpec((dil, tm // dil, n), lambda i: (0, i, 0)),
        out_shape=jax.ShapeDtypeStruct((dil, t // dil, n), MXU_DTYPE),
        scratch_shapes=[pltpu.VMEM((n // LANES, tm, LANES), F32)],
        compiler_params=_params("parallel"),
        name=name,
    )(x2d, gain, w, partner, tabs[0], tabs[1], head_gain)


def _dilated_kernel(span, dil, q_ref, kc_ref, kp_ref, vc_ref, vp_ref, o_ref, lse_ref):
    i = pl.program_id(1)
    qi = lax.broadcasted_iota(jnp.int32, (A_BLOCK, 2 * A_BLOCK), 0) + A_BLOCK
    kj = lax.broadcasted_iota(jnp.int32, (A_BLOCK, 2 * A_BLOCK), 1)
    dist = qi - kj
    min_key = jnp.where(i > 0, 0, A_BLOCK)
    bias = jnp.where((dist >= 0) & (dist <= span) & (kj >= min_key), 0.0, NEG_BIG)

    def residue(r, carry):
        rows = pl.ds(r, A_BLOCK, stride=dil) if dil > 1 else slice(None)
        for h in range(A_HEADS_PER_GROUP):
            hs = slice(h * HEAD_DIM, (h + 1) * HEAD_DIM)
            q = q_ref[r, :, hs]
            k2 = jnp.concatenate([kp_ref[r, :, hs], kc_ref[r, :, hs]], axis=0)
            v2 = jnp.concatenate([vp_ref[r, :, hs], vc_ref[r, :, hs]], axis=0)
            s = lax.dot_general(q, k2, (((1,), (1,)), ((), ())), preferred_element_type=F32) + bias
            m = jnp.max(s, axis=-1, keepdims=True)
            p = jnp.exp(s - m)
            l = jnp.sum(p, axis=-1, keepdims=True)
            o = jnp.dot(p.astype(v2.dtype), v2, preferred_element_type=F32)
            o_ref[h, rows, :] = o / l
            lse_ref[h, rows, :] = jnp.broadcast_to(m + jnp.log(l), (A_BLOCK, HEAD_DIM))
        return carry

    lax.fori_loop(0, dil, residue, 0)


def _dilated_group(qkv, bsz, s, window, dil):
    n = s // dil
    nb = n // A_BLOCK
    blk = (dil, A_BLOCK, A_GROUP_WIDTH)

    def prev(i):
        return jnp.maximum(i - 1, 0)

    in_specs = [
        pl.BlockSpec(blk, lambda b, i: (0, b * nb + i, 0)),
        pl.BlockSpec(blk, lambda b, i: (0, b * nb + i, 1)),
        pl.BlockSpec(blk, lambda b, i: (0, b * nb + prev(i), 1)),
        pl.BlockSpec(blk, lambda b, i: (0, b * nb + i, 2)),
        pl.BlockSpec(blk, lambda b, i: (0, b * nb + prev(i), 2)),
    ]
    out_spec = pl.BlockSpec((A_HEADS_PER_GROUP, dil * A_BLOCK, HEAD_DIM), lambda b, i: (0, b * nb + i, 0))
    out_sds = jax.ShapeDtypeStruct((A_HEADS_PER_GROUP, bsz * s, HEAD_DIM), F32)
    return pl.pallas_call(
        functools.partial(_dilated_kernel, window // dil, dil),
        grid=(bsz, nb),
        in_specs=in_specs,
        out_specs=[out_spec, out_spec],
        out_shape=[out_sds, out_sds],
        compiler_params=_params("parallel", "arbitrary"),
        name=f"dilated_attn_d{dil}",
    )(qkv, qkv, qkv, qkv, qkv)


def _merge_groups_kernel(o0, o1, o2, l0, l1, l2, out_ref):
    for h in range(A_HEADS_PER_GROUP):
        a, b, c = l0[h], l1[h], l2[h]
        m = jnp.maximum(jnp.maximum(a, b), c)
        ea, eb, ec = jnp.exp(a - m), jnp.exp(b - m), jnp.exp(c - m)
        num = ea * o0[h] + eb * o1[h] + ec * o2[h]
        out_ref[:, h * HEAD_DIM:(h + 1) * HEAD_DIM] = (num / (ea + eb + ec)).astype(out_ref.dtype)


def _merge_groups(os_, lses):
    t = os_[0].shape[1]
    tm = min(t, 1024)
    spec = pl.BlockSpec((A_HEADS_PER_GROUP, tm, HEAD_DIM), lambda i: (0, i, 0))
    return pl.pallas_call(
        _merge_groups_kernel,
        grid=(t // tm,),
        in_specs=[spec] * 6,
        out_specs=pl.BlockSpec((tm, A_GROUP_WIDTH), lambda i: (i, 0)),
        out_shape=jax.ShapeDtypeStruct((t, A_GROUP_WIDTH), MXU_DTYPE),
        compiler_params=_params("parallel"),
        name="dilated_merge",
    )(*os_, *lses)


def _tree_row_sum(x, group):
    parts = [x[g * group:(g + 1) * group, :] for g in range(x.shape[0] // group)]
    while len(parts) > 1:
        parts = [a + b for a, b in zip(parts[0::2], parts[1::2])]
    return parts[0]


def _sparse_kernel(topk, idx_scale, qit_ref, wt_ref, ki_ref, qbt_ref, kb_ref, vt_ref,
                   out_ref, keys_ref, hi_ref, acc_ref, m_ref, s_idx0, s_idx1, s_att0, s_att1,
                   p0, p1, al0, al1, bias_buf):
    qb = pl.program_id(1)
    t0 = qb * QB_BLOCK
    n_pairs = (t0 + QB_BLOCK + 2 * KEY_TILE - 1) // (2 * KEY_TILE)
    n_tiles = 2 * n_pairs
    last_tile = n_tiles - 1
    lane_q = lax.broadcasted_iota(jnp.int32, (KEY_TILE, QB_BLOCK), 1) + t0
    row_id = lax.broadcasted_iota(jnp.int32, (KEY_TILE, QB_BLOCK), 0)
    query_pos = lax.broadcasted_iota(jnp.int32, (1, QB_BLOCK), 1) + t0
    few_keys = query_pos < topk
    wts = wt_ref[...] * idx_scale

    def tile_start(kt):
        return pl.multiple_of(kt * KEY_TILE, KEY_TILE)

    def idx_matmul(kt, buf):
        k_tile = ki_ref[pl.ds(tile_start(kt), KEY_TILE), 0:IDX_DIM]
        buf[...] = jnp.dot(k_tile, qit_ref[...], preferred_element_type=F32)

    def idx_reduce(kt, buf):
        r0 = tile_start(kt)
        acc = jnp.zeros((KEY_TILE, QB_BLOCK), F32)
        for h in range(IDX_HEADS):
            acc = acc + jnp.maximum(buf[:, h * QB_BLOCK:(h + 1) * QB_BLOCK], 0.0) * wts[h:h + 1, :]
        acc = jnp.where(jnp.abs(acc) < F32_MIN_NORMAL, 0.0, acc)
        acc = jnp.where(row_id + r0 <= lane_q, acc, -jnp.inf)
        bits = pltpu.bitcast(acc, jnp.int32)
        keys_ref[pl.ds(r0, KEY_TILE), :] = bits ^ ((bits >> 31) & 0x7FFFFFFF)
        hi_ref[pl.ds(r0, KEY_TILE), :] = pltpu.bitcast(bits & HI16_MASK, F32).astype(jnp.bfloat16)

    idx_matmul(0, s_idx0)

    def score_pair(i, carry):
        idx_matmul(2 * i + 1, s_idx1)
        idx_reduce(2 * i, s_idx0)
        idx_matmul(jnp.minimum(2 * i + 2, last_tile), s_idx0)
        idx_reduce(2 * i + 1, s_idx1)
        return carry

    lax.fori_loop(0, n_pairs, score_pair, 0)

    def sum_over_keys(partial, acc_rows, dtype):
        per_chunk = COUNT_CHUNK // KEY_TILE
        n_chunks = n_tiles // per_chunk

        def chunk(c, acc):
            return acc + partial(pl.multiple_of(c * COUNT_CHUNK, COUNT_CHUNK), COUNT_CHUNK)

        def tile(kt, acc):
            return acc + partial(tile_start(kt), KEY_TILE)

        acc = lax.fori_loop(0, n_chunks, chunk, jnp.zeros((acc_rows, QB_BLOCK), dtype))
        acc = lax.fori_loop(n_chunks * per_chunk, n_tiles, tile, acc)
        return jnp.sum(acc, axis=0, keepdims=True)

    def count_prefix_ge(prefix):
        prefix = jnp.where((prefix > 0) & (prefix < BF16_MIN_NORMAL), BF16_MIN_NORMAL, prefix)
        fbits = (prefix ^ ((prefix >> 15) & 0x7FFF)) << 16
        cand = pltpu.bitcast(fbits, F32).astype(jnp.bfloat16)

        def partial(r0, rows):
            hit = jnp.where(hi_ref[pl.ds(r0, rows), :] >= cand,
                            jnp.ones((rows, QB_BLOCK), jnp.bfloat16),
                            jnp.zeros((rows, QB_BLOCK), jnp.bfloat16))
            return _tree_row_sum(hit, BF16_ROWS).astype(F32)

        return sum_over_keys(partial, BF16_ROWS, F32)

    want = jnp.float32(topk)
    zero_prefix = jnp.zeros((1, QB_BLOCK), jnp.int32)
    cnt0 = count_prefix_ge(zero_prefix)
    take0 = cnt0 >= want
    prefix = jnp.where(take0, zero_prefix, -(2 ** 15))
    cnt = jnp.where(take0, cnt0, jnp.float32(2 ** 24))

    def prefix_step(i, carry):
        prefix, cnt = carry
        cand = prefix | lax.shift_left(jnp.int32(1), 14 - i)
        c = count_prefix_ge(cand)
        take = c >= want
        return jnp.where(take, cand, prefix), jnp.where(take, c, cnt)

    prefix, cnt = lax.fori_loop(0, 15, prefix_step, (prefix, cnt))

    def count_where(pred):
        def partial(r0, rows):
            rid = lax.broadcasted_iota(jnp.int32, (rows, QB_BLOCK), 0) + r0
            hit = jnp.where(pred(keys_ref[pl.ds(r0, rows), :], rid), 1, 0)
            return _tree_row_sum(hit, SUBLANES)
        return sum_over_keys(partial, SUBLANES, jnp.int32)

    def count_ge(cand):
        return count_where(lambda k, rid: k >= cand)

    def unresolved(cnt):
        return jnp.max(jnp.where(few_keys | (cnt == topk), 0, 1)) > 0

    def low_cond(carry):
        bit, _, cnt = carry
        return (bit >= 0) & unresolved(cnt)

    def low_step(carry):
        bit, tau, cnt = carry
        cand = tau | lax.shift_left(jnp.int32(1), bit)
        c = count_ge(cand)
        take = c >= topk
        return bit - 1, jnp.where(take, cand, tau), jnp.where(take, c, cnt)

    _, tau, n_ge = lax.while_loop(low_cond, low_step,
                                  (jnp.int32(15), prefix << 16, cnt.astype(jnp.int32)))
    tau = jnp.where(few_keys, KEY_NEG_INF + 1, tau)

    @pl.when(jnp.max(jnp.where(few_keys, 0, n_ge)) > topk)
    def _():
        need = topk - count_ge(tau + 1)

        def idx_step(i, last):
            cand = last | lax.shift_left(jnp.int32(1), 30 - i)
            below = count_where(lambda k, rid: (k == tau) & (rid < cand))
            return jnp.where(below < need, cand, last)

        last = lax.fori_loop(0, 31, idx_step, jnp.zeros((1, QB_BLOCK), jnp.int32))

        def demote(kt, carry):
            r0 = tile_start(kt)
            k = keys_ref[pl.ds(r0, KEY_TILE), :]
            drop = (k == tau) & (row_id + r0 > last)
            keys_ref[pl.ds(r0, KEY_TILE), :] = jnp.where(drop, tau - 1, k)
            return carry

        lax.fori_loop(0, n_tiles, demote, 0)

    def att_matmul(kt, s_buf):
        s_buf[...] = jnp.dot(kb_ref[pl.ds(tile_start(kt), KEY_TILE), :], qbt_ref[...],
                             preferred_element_type=F32)

    def softmax_tile(kt, s_buf, p_buf, al_buf):
        sel = keys_ref[pl.ds(tile_start(kt), KEY_TILE), :] >= tau
        bias_buf[...] = jnp.where(sel, 0.0, NEG_BIG)
        for h in range(B_HEADS):
            hs = slice(h * QB_BLOCK, (h + 1) * QB_BLOCK)
            s = s_buf[:, hs] + bias_buf[...]
            m_old = m_ref[:, hs]
            m_new = jnp.maximum(m_old, jnp.max(s, axis=0, keepdims=True))
            m_ref[:, hs] = m_new
            al_buf[:, hs] = jnp.exp2(m_old - m_new)
            p_buf[:, hs] = jnp.exp2(s - m_new).astype(p_buf.dtype)

    def value_matmul(kt, p_buf, al_buf):
        pv = jnp.dot(vt_ref[:, pl.ds(tile_start(kt), KEY_TILE)], p_buf[...],
                     preferred_element_type=F32)
        acc_ref[...] = acc_ref[...] * al_buf[...] + pv

    m_ref[...] = jnp.full(m_ref.shape, NEG_BIG, F32)
    acc_ref[...] = jnp.zeros(acc_ref.shape, F32)
    p1[...] = jnp.zeros(p1.shape, p1.dtype)
    al1[...] = jnp.ones(al1.shape, F32)
    att_matmul(0, s_att0)

    def attend_pair(i, carry):
        att_matmul(2 * i + 1, s_att1)
        value_matmul(jnp.maximum(2 * i - 1, 0), p1, al1)
        softmax_tile(2 * i, s_att0, p0, al0)
        att_matmul(jnp.minimum(2 * i + 2, last_tile), s_att0)
        value_matmul(2 * i, p0, al0)
        softmax_tile(2 * i + 1, s_att1, p1, al1)
        return carry

    lax.fori_loop(0, n_pairs, attend_pair, 0)
    value_matmul(last_tile, p1, al1)

    for h in range(B_HEADS):
        hs = slice(h * QB_BLOCK, (h + 1) * QB_BLOCK)
        o_t = acc_ref[0:HEAD_DIM, hs] / acc_ref[HEAD_DIM:HEAD_DIM + 1, hs]
        out_ref[:, h * HEAD_DIM:(h + 1) * HEAD_DIM] = o_t.T.astype(out_ref.dtype)


def _sparse_mixer(bproj, idx, wi_t, bsz, s):
    nqb = s // QB_BLOCK
    topk = min(B_TOPK_MAX, s // 4)
    idx_scale = (IDX_HEADS ** -0.5) * (IDX_DIM ** -0.5)
    nq = B_HEADS * QB_BLOCK
    qi = idx[:, :IDX_HEADS * IDX_DIM].reshape(bsz, nqb, QB_BLOCK, IDX_HEADS, IDX_DIM)
    qit = qi.transpose(0, 1, 4, 3, 2).reshape(bsz, nqb, IDX_DIM, IDX_HEADS * QB_BLOCK)
    qbq = bproj[:, :B_HEADS * HEAD_DIM].reshape(bsz, nqb, QB_BLOCK, B_HEADS, HEAD_DIM)
    qbt = qbq.transpose(0, 1, 4, 3, 2).reshape(bsz, nqb, HEAD_DIM, nq)
    vb = bproj[:, VB_COL:VB_COL + HEAD_DIM].reshape(bsz, s, HEAD_DIM)
    vt = jnp.concatenate([vb.transpose(0, 2, 1),
                          jnp.ones((bsz, 1, s), vb.dtype),
                          jnp.zeros((bsz, SUBLANES - 1, s), vb.dtype)], axis=1)
    v_rows = HEAD_DIM + SUBLANES
    b3 = bproj.reshape(bsz, s, B_COLS)
    idx3 = idx.reshape(bsz, s, IDX_COLS)
    kb_blk = KB_COL // LANES
    ki_blk = (IDX_HEADS * IDX_DIM) // LANES
    out = pl.pallas_call(
        functools.partial(_sparse_kernel, topk, idx_scale),
        grid=(bsz, nqb),
        in_specs=[
            pl.BlockSpec((None, None, IDX_DIM, IDX_HEADS * QB_BLOCK), lambda b, q: (b, q, 0, 0)),
            pl.BlockSpec((None, None, IDX_HEADS, QB_BLOCK), lambda b, q: (b, q, 0, 0)),
            pl.BlockSpec((None, s, LANES), lambda b, q: (b, 0, ki_blk)),
            pl.BlockSpec((None, None, HEAD_DIM, nq), lambda b, q: (b, q, 0, 0)),
            pl.BlockSpec((None, s, LANES), lambda b, q: (b, 0, kb_blk)),
            pl.BlockSpec((None, v_rows, s), lambda b, q: (b, 0, 0)),
        ],
        out_specs=pl.BlockSpec((None, QB_BLOCK, B_HEADS * HEAD_DIM), lambda b, q: (b, q, 0)),
        out_shape=jax.ShapeDtypeStruct((bsz, s, B_HEADS * HEAD_DIM), MXU_DTYPE),
        scratch_shapes=[pltpu.VMEM((s, QB_BLOCK), jnp.int32),
                        pltpu.VMEM((s, QB_BLOCK), jnp.bfloat16),
                        pltpu.VMEM((v_rows, nq), F32),
                        pltpu.VMEM((1, nq), F32),
                        pltpu.VMEM((KEY_TILE, IDX_HEADS * QB_BLOCK), F32),
                        pltpu.VMEM((KEY_TILE, IDX_HEADS * QB_BLOCK), F32),
                        pltpu.VMEM((KEY_TILE, nq), F32),
                        pltpu.VMEM((KEY_TILE, nq), F32),
                        pltpu.VMEM((KEY_TILE, nq), MXU_DTYPE),
                        pltpu.VMEM((KEY_TILE, nq), MXU_DTYPE),
                        pltpu.VMEM((1, nq), F32),
                        pltpu.VMEM((1, nq), F32),
                        pltpu.VMEM((KEY_TILE, QB_BLOCK), F32)],
        compiler_params=_params("parallel", "arbitrary"),
        name="sparse_mixer",
    )(qit, wi_t, idx3, qbt, b3, vt)
    return out.reshape(bsz * s, B_HEADS * HEAD_DIM)


def _lru_kernel(ts, xc_ref, yc_ref, cw_ref, cb_ref, wa_ref, wx_ref, ba_ref, bx_ref, lam_ref,
                o_ref, xbuf, a_buf, b_buf, h_state):
    halo = SUBLANES

    @pl.when(pl.program_id(1) == 0)
    def _():
        xbuf[0:halo, :] = jnp.zeros((halo, xbuf.shape[1]), F32)
        h_state[...] = jnp.zeros(h_state.shape, F32)

    xbuf[halo:halo + ts, :] = xc_ref[...]
    xc = cb_ref[...] + cw_ref[0:1, :] * xbuf[halo - 3:halo - 3 + ts, :]
    for tap in range(1, LRU_CONV_W):
        lo = halo - (LRU_CONV_W - 1) + tap
        xc = xc + cw_ref[tap:tap + 1, :] * xbuf[lo:lo + ts, :]
    xbuf[0:halo, :] = xc_ref[ts - halo:ts, :]

    z = -lam_ref[...]
    softplus = jnp.maximum(z, 0.0) + jnp.log1p(jnp.exp(-jnp.abs(z)))
    bw = xc.shape[1] // LRU_BLOCKS
    for blk in range(LRU_BLOCKS):
        cs = slice(blk * bw, (blk + 1) * bw)
        xs = xc[:, cs]
        xs_m = xs.astype(MXU_DTYPE)
        r = _sigmoid(jnp.dot(xs_m, wa_ref[blk], preferred_element_type=F32) + ba_ref[:, cs])
        gate_i = _sigmoid(jnp.dot(xs_m, wx_ref[blk], preferred_element_type=F32) + bx_ref[:, cs])
        log_a = (-LRU_C) * r * softplus[:, cs]
        a = jnp.exp(log_a)
        a_buf[:, cs] = a
        b_buf[:, cs] = jnp.sqrt(-jnp.tanh(log_a) * (a * a + 1.0)) * gate_i * xs

    def step(t, h):
        h = a_buf[pl.ds(t, 1), :] * h + b_buf[pl.ds(t, 1), :]
        b_buf[pl.ds(t, 1), :] = h
        return h

    h_state[0:1, :] = lax.fori_loop(0, ts, step, h_state[0:1, :], unroll=8)
    o_ref[...] = (b_buf[...] * _gelu_tanh(yc_ref[...])).astype(o_ref.dtype)


def _lru_mixer(feat, bsz, s, width, conv_w, conv_b, wa, ba, wx, bx, lam):
    ts = min(s, 512)
    nt = s // ts
    row = lambda a: a.reshape(1, width).astype(F32)
    vec = pl.BlockSpec((1, width), lambda b, t: (0, 0))
    wspec = pl.BlockSpec(wa.shape, lambda b, t: (0, 0, 0))
    return pl.pallas_call(
        functools.partial(_lru_kernel, ts),
        grid=(bsz, nt),
        in_specs=[pl.BlockSpec((ts, width), lambda b, t: (b * nt + t, 0)),
                  pl.BlockSpec((ts, width), lambda b, t: (b * nt + t, 1)),
                  pl.BlockSpec((LRU_CONV_W, width), lambda b, t: (0, 0)),
                  vec, wspec, wspec, vec, vec, vec],
        out_specs=pl.BlockSpec((ts, width), lambda b, t: (b * nt + t, 0)),
        out_shape=jax.ShapeDtypeStruct((bsz * s, width), MXU_DTYPE),
        scratch_shapes=[pltpu.VMEM((ts + SUBLANES, width), F32),
                        pltpu.VMEM((ts, width), F32),
                        pltpu.VMEM((ts, width), F32),
                        pltpu.VMEM((SUBLANES, width), F32)],
        compiler_params=_params("parallel", "arbitrary"),
        name="rglru_mixer",
    )(feat, feat, conv_w.astype(F32), row(conv_b), wa.astype(MXU_DTYPE), wx.astype(MXU_DTYPE),
      row(ba), row(bx), row(lam))


def _gate_merge_kernel(oa, ob, oc, wa, wb, wc, g0, g1, g2, b0, b1, b2, out_ref):
    acc = _sigmoid(g0[...] + b0[...]) * jnp.dot(oa[...], wa[...], preferred_element_type=F32)
    acc = acc + _sigmoid(g1[...] + b1[...]) * jnp.dot(ob[...], wb[...], preferred_element_type=F32)
    acc = acc + _sigmoid(g2[...] + b2[...]) * jnp.dot(oc[...], wc[...], preferred_element_type=F32)
    out_ref[...] = acc.astype(out_ref.dtype)


def _gate_merge(out_a, out_b, out_c, wpa, wpb, wpc, feat, gate_col0, b_gate_row, d_model):
    t = out_a.shape[0]
    tm = min(t, 1024)
    tn = 512
    nj = d_model // tn
    g_blk = gate_col0 // tn

    def act(a):
        return pl.BlockSpec((tm, a.shape[1]), lambda i, j: (i, 0))

    def wgt(w):
        return pl.BlockSpec((w.shape[0], tn), lambda i, j: (0, j))

    gates = [pl.BlockSpec((tm, tn), functools.partial(lambda br, i, j: (i, g_blk + br * nj + j), br))
             for br in range(N_BRANCH)]
    biases = [pl.BlockSpec((1, tn), functools.partial(lambda br, i, j: (0, br * nj + j), br))
              for br in range(N_BRANCH)]
    return pl.pallas_call(
        _gate_merge_kernel,
        grid=(t // tm, nj),
        in_specs=[act(out_a), act(out_b), act(out_c), wgt(wpa), wgt(wpb), wgt(wpc)] + gates + biases,
        out_specs=pl.BlockSpec((tm, tn), lambda i, j: (i, j)),
        out_shape=jax.ShapeDtypeStruct((t, d_model), MXU_DTYPE),
        compiler_params=_params("parallel", "arbitrary"),
        name="gate_merge",
    )(out_a, out_b, out_c, wpa, wpb, wpc, feat, feat, feat, b_gate_row, b_gate_row, b_gate_row)


def _matmul_residual_kernel(a_ref, w_ref, r_ref, o_ref):
    o_ref[...] = r_ref[...] + jnp.dot(a_ref[...], w_ref[...], preferred_element_type=F32)


def _matmul_residual(a, w, resid, name):
    t, k = a.shape
    n = w.shape[1]
    tm = min(t, 1024)
    tn = 512
    return pl.pallas_call(
        _matmul_residual_kernel,
        grid=(t // tm, n // tn),
        in_specs=[pl.BlockSpec((tm, k), lambda i, j: (i, 0)),
                  pl.BlockSpec((k, tn), lambda i, j: (0, j)),
                  pl.BlockSpec((tm, tn), lambda i, j: (i, j))],
        out_specs=pl.BlockSpec((tm, tn), lambda i, j: (i, j)),
        out_shape=jax.ShapeDtypeStruct((t, n), F32),
        compiler_params=_params("parallel", "arbitrary"),
        name=name,
    )(a, w, resid)


def _ffn_up_kernel(tm, tiles_per_seq, x_ref, xh_ref, g_ref, wa_ref, wb_ref, cwa_ref, cwb_ref,
                   cba_ref, cbb_ref, o_ref, h_ref, hh_ref, ua_buf, ub_buf):
    halo = SUBLANES

    def normed(ref):
        xf = ref[...]
        ms = jnp.mean(xf * xf, axis=-1, keepdims=True)
        return (xf * lax.rsqrt(ms + NORM_EPS) * g_ref[...]).astype(MXU_DTYPE)

    @pl.when(pl.program_id(1) == 0)
    def _():
        h_ref[...] = normed(x_ref)
        hh_ref[...] = normed(xh_ref)

    keep = jnp.where(pl.program_id(0) % tiles_per_seq == 0, 0.0, 1.0)

    def conv_half(w_ref, cw_ref, cb_ref, buf):
        buf[0:halo, :] = keep * jnp.dot(hh_ref[...], w_ref[...], preferred_element_type=F32)
        buf[halo:halo + tm, :] = jnp.dot(h_ref[...], w_ref[...], preferred_element_type=F32)
        y = cb_ref[...] + cw_ref[0:1, :] * buf[halo - 2:halo - 2 + tm, :]
        for tap in range(1, FFN_CONV_W):
            lo = halo - (FFN_CONV_W - 1) + tap
            y = y + cw_ref[tap:tap + 1, :] * buf[lo:lo + tm, :]
        return y

    ya = conv_half(wa_ref, cwa_ref, cba_ref, ua_buf)
    yb = conv_half(wb_ref, cwb_ref, cbb_ref, ub_buf)
    o_ref[...] = (_gelu_tanh(ya) * yb).astype(o_ref.dtype)


def _ffn_up(x2d, s, gain, w_up, conv_w, conv_b_row):
    t, d = x2d.shape
    d_ff = w_up.shape[1] // 2
    tm = min(s, 1024)
    tn = 512
    nj = d_ff // tn
    hb = tm // SUBLANES
    return pl.pallas_call(
        functools.partial(_ffn_up_kernel, tm, s // tm),
        grid=(t // tm, nj),
        in_specs=[pl.BlockSpec((tm, d), lambda i, j: (i, 0)),
                  pl.BlockSpec((SUBLANES, d), lambda i, j: (jnp.maximum(i * hb - 1, 0), 0)),
                  pl.BlockSpec((1, d), lambda i, j: (0, 0)),
                  pl.BlockSpec((d, tn), lambda i, j: (0, j)),
                  pl.BlockSpec((d, tn), lambda i, j: (0, nj + j)),
                  pl.BlockSpec((FFN_CONV_W, tn), lambda i, j: (0, j)),
                  pl.BlockSpec((FFN_CONV_W, tn), lambda i, j: (0, nj + j)),
                  pl.BlockSpec((1, tn), lambda i, j: (0, j)),
                  pl.BlockSpec((1, tn), lambda i, j: (0, nj + j))],
        out_specs=pl.BlockSpec((tm, tn), lambda i, j: (i, j)),
        out_shape=jax.ShapeDtypeStruct((t, d_ff), MXU_DTYPE),
        scratch_shapes=[pltpu.VMEM((tm, d), MXU_DTYPE),
                        pltpu.VMEM((SUBLANES, d), MXU_DTYPE),
                        pltpu.VMEM((tm + SUBLANES, tn), F32),
                        pltpu.VMEM((tm + SUBLANES, tn), F32)],
        compiler_params=_params("parallel", "arbitrary"),
        name="ffn_up_conv_gate",
    )(x2d, x2d, gain, w_up, w_up, conv_w, conv_w, conv_b_row, conv_b_row)


def _pad_cols(parts, total, dtype):
    width = sum(p.shape[1] for p in parts)
    rows = parts[0].shape[0]
    if total > width:
        parts = list(parts) + [jnp.zeros((rows, total - width), parts[0].dtype)]
    return jnp.concatenate(parts, axis=1).astype(dtype)


def _temporal_block(x2d, bsz, s, tabs_head, tabs_idx, w_in, b_gate, qn_a, kn_a, qn_b, kn_b,
                    lru_conv_w, lru_conv_b, lru_wa, lru_ba, lru_wx, lru_bx, lru_lambda,
                    w_proj_a, w_proj_b, w_proj_c, w_out, norm_mix):
    d = x2d.shape[1]
    a_w = A_HEADS * HEAD_DIM
    b_w = B_HEADS * HEAD_DIM
    sizes = (a_w, a_w, a_w, b_w, HEAD_DIM, HEAD_DIM, IDX_HEADS * IDX_DIM, IDX_DIM, IDX_HEADS,
             d, d, N_BRANCH * d)
    offs = np.concatenate([[0], np.cumsum(sizes)])
    col = lambda k: w_in[:, offs[k]:offs[k + 1]]
    w_qa, w_ka, w_va, w_qb, w_kb, w_vb, w_qi, w_ki, w_wi, w_xc, w_yc, w_g = [col(k) for k in range(12)]

    gain_row = norm_mix.reshape(1, d).astype(F32)
    att_scale = HEAD_DIM ** -0.5
    ones = lambda k: jnp.ones((k,), F32)
    part_head = _partner_matrix(HEAD_DIM, ROT_DIM // 2)
    part_idx = _partner_matrix(IDX_DIM, IDX_ROT // 2)
    gw = A_GROUP_WIDTH

    a_gain = jnp.concatenate([jnp.tile(qn_a.astype(F32) * att_scale, A_HEADS_PER_GROUP),
                              jnp.tile(kn_a.astype(F32), A_HEADS_PER_GROUP), ones(gw)]).reshape(1, 3 * gw)
    a_kinds = [NORM_ROPE] * 4 + [PLAIN] * 2
    os_, lses = [], []
    for g, (window, dil) in enumerate(DIL_GROUPS):
        gs = slice(g * gw, (g + 1) * gw)
        w_g3 = jnp.concatenate([w_qa[:, gs], w_ka[:, gs], w_va[:, gs]], axis=1).astype(MXU_DTYPE)
        qkv = _norm_proj_heads(x2d, gain_row, w_g3, part_head, tabs_head, a_gain, a_kinds,
                               dil=dil, name=f"proj_a_d{dil}")
        o, lse = _dilated_group(qkv, bsz, s, window, dil)
        os_.append(o)
        lses.append(lse)
    out_a = _merge_groups(os_, lses)

    b_gain = jnp.concatenate([jnp.tile(qn_b.astype(F32) * (att_scale * LOG2_E), B_HEADS),
                              kn_b.astype(F32), ones(B_COLS - b_w - HEAD_DIM)]).reshape(1, B_COLS)
    zeros = jnp.zeros((d, HEAD_DIM), w_in.dtype)
    w_b = jnp.concatenate([w_qb, w_kb, zeros, w_vb, zeros], axis=1).astype(MXU_DTYPE)
    b_kinds = [NORM_ROPE] * (b_w // EP_CHUNK + 1) + [PLAIN]
    bproj = _norm_proj_heads(x2d, gain_row, w_b, part_head, tabs_head, b_gain, b_kinds,
                             dil=1, name="proj_b").reshape(bsz * s, B_COLS)
    idx = _norm_proj_heads(x2d, gain_row, _pad_cols([w_qi, w_ki], IDX_COLS, MXU_DTYPE), part_idx,
                           tabs_idx, jnp.ones((1, IDX_COLS), F32), [ROPE] * (IDX_COLS // EP_CHUNK),
                           dil=1, name="proj_idx").reshape(bsz * s, IDX_COLS)
    gate_col0 = 2 * d
    wi_col0 = gate_col0 + N_BRANCH * d
    feat_cols = -(-(wi_col0 + IDX_HEADS) // 512) * 512
    feat = _norm_proj(x2d, gain_row, _pad_cols([w_xc, w_yc, w_g, w_wi], feat_cols, MXU_DTYPE),
                      tn=512, out_dtype=F32, name="proj_feat")
    wi = feat[:, wi_col0:wi_col0 + IDX_HEADS].reshape(bsz, s // QB_BLOCK, QB_BLOCK, IDX_HEADS)
    out_b = _sparse_mixer(bproj, idx, wi.transpose(0, 1, 3, 2), bsz, s)

    out_c = _lru_mixer(feat, bsz, s, d, lru_conv_w, lru_conv_b, lru_wa, lru_ba, lru_wx, lru_bx,
                       lru_lambda)

    merged = _gate_merge(out_a, out_b, out_c, w_proj_a.astype(MXU_DTYPE),
                         w_proj_b.astype(MXU_DTYPE), w_proj_c.astype(MXU_DTYPE), feat, gate_col0,
                         b_gate.reshape(1, N_BRANCH * d).astype(F32), d)
    return _matmul_residual(merged, w_out.astype(MXU_DTYPE), x2d, "out_proj_residual")


def _conv_ffn_block(x2d, s, norm_ffn, ffn_w_up, ffn_conv_w, ffn_conv_b, ffn_w_down):
    d = x2d.shape[1]
    act = _ffn_up(x2d, s, norm_ffn.reshape(1, d).astype(F32), ffn_w_up.astype(MXU_DTYPE),
                  ffn_conv_w.astype(F32), ffn_conv_b.reshape(1, -1).astype(F32))
    return _matmul_residual(act, ffn_w_down.astype(MXU_DTYPE), x2d, "ffn_down_residual")


def kernel(x, positions, w_in, b_gate, qn_a, kn_a, qn_b, kn_b, lru_conv_w, lru_conv_b, lru_wa,
           lru_ba, lru_wx, lru_bx, lru_lambda, w_proj_a, w_proj_b, w_proj_c, w_out, norm_mix,
           norm_ffn, ffn_w_up, ffn_conv_w, ffn_conv_b, ffn_w_down):
    bsz, s, d = x.shape
    assert s % (DIL_GROUPS[-1][1] * A_BLOCK) == 0 and s % (2 * KEY_TILE) == 0
    x2d = x.reshape(bsz * s, d)
    pos_col = positions.astype(F32).reshape(bsz * s, 1)
    tabs_head = _rope_tables(pos_col, ROT_DIM, HEAD_DIM)
    tabs_idx = _rope_tables(pos_col, IDX_ROT, IDX_DIM)
    for layer in range(w_in.shape[0]):
        x2d = _temporal_block(
            x2d, bsz, s, tabs_head, tabs_idx, w_in[layer], b_gate[layer], qn_a[layer],
            kn_a[layer], qn_b[layer], kn_b[layer], lru_conv_w[layer], lru_conv_b[layer],
            lru_wa[layer], lru_ba[layer], lru_wx[layer], lru_bx[layer], lru_lambda[layer],
            w_proj_a[layer], w_proj_b[layer], w_proj_c[layer], w_out[layer], norm_mix[layer])
        x2d = _conv_ffn_block(x2d, s, norm_ffn[layer], ffn_w_up[layer], ffn_conv_w[layer],
                              ffn_conv_b[layer], ffn_w_down[layer])
    return x2d.reshape(bsz, s, d)
```

```python
import functools
import math

import numpy as np
import jax
import jax.numpy as jnp
from jax import lax
from jax.experimental import pallas as pl
from jax.experimental.pallas import tpu as pltpu

F32 = jnp.float32
MXU_DTYPE = jnp.bfloat16
SOFTMAX_DTYPE = jnp.bfloat16

LANES = 128
SUBLANES = 8
VMEM_LIMIT_BYTES = 58 * 1024 * 1024

HEAD_DIM = 128
ROPE_THETA = 500000.0
ROT_DIM = HEAD_DIM // 4
NORM_EPS = 1e-6
DIL_GROUPS = ((128, 1), (512, 4), (2048, 16))
A_HEADS_PER_GROUP = 4
A_HEADS = A_HEADS_PER_GROUP * len(DIL_GROUPS)
A_BLOCK = 128
B_HEADS = 12
B_TOPK_MAX = 256
IDX_HEADS = 16
IDX_DIM = 64
IDX_ROT = IDX_DIM // 4
LRU_BLOCKS = 16
LRU_C = 8.0
LRU_CONV_W = 4
FFN_CONV_W = 3
N_BRANCH = 3
A_GROUP_WIDTH = A_HEADS_PER_GROUP * HEAD_DIM

B_COLS = 2048
KB_COL = 1536
VB_COL = 1792
IDX_COLS = 1280
EP_CHUNK = 256
PROJ_GROUP = 512
QB_BLOCK = 128
KEY_TILE = 256
STEPS_PER_BODY = 4
INT_MIN = -(2 ** 31)
KEY_NEG_INF = INT_MIN + 0x7FFFFF
HI16_MASK = -65536
BF16_ROWS = 16
COUNT_CHUNK = 1024
BF16_MIN_NORMAL = 0x80
F32_MIN_NORMAL = 1.1754943508222875e-38
NEG_BIG = -(2.0 ** 100)
LOG2_E = 1.4426950408889634


def _params(*semantics):
    return pltpu.CompilerParams(dimension_semantics=semantics,
                                vmem_limit_bytes=VMEM_LIMIT_BYTES)


def _gelu_tanh(x):
    return 0.5 * x * (1.0 + jnp.tanh(0.7978845608028654 * (x + 0.044715 * (x * x * x))))


def _sigmoid(x):
    return 1.0 / (1.0 + jnp.exp(-x))


def _rope_tables_kernel(pos_ref, par_ref, c_ref, s_ref):
    ang = pos_ref[...] * par_ref[0:1, :]
    c_ref[...] = jnp.where(par_ref[1:2, :] > 0.0, jnp.cos(ang), 1.0)
    s_ref[...] = jnp.sin(ang) * par_ref[2:3, :]


def _rope_tables(pos_col, rot_dim, period):
    t = pos_col.shape[0]
    half = rot_dim // 2
    freq = ROPE_THETA ** (-jnp.arange(half, dtype=F32) / half)
    lane = np.arange(LANES) % period
    in_rot = lane < rot_dim
    freq_lane = jnp.where(jnp.asarray(in_rot), freq[lane % half], 0.0)
    sign = np.where(lane < half, -1.0, np.where(in_rot, 1.0, 0.0))
    rows = [freq_lane, jnp.asarray(in_rot, F32), jnp.asarray(sign, F32)]
    par = jnp.stack(rows + [jnp.zeros((LANES,), F32)] * 5, axis=0)
    tm = min(t, 2048)
    spec = pl.BlockSpec((tm, LANES), lambda i: (i, 0))
    return pl.pallas_call(
        _rope_tables_kernel,
        grid=(t // tm,),
        in_specs=[pl.BlockSpec((tm, 1), lambda i: (i, 0)),
                  pl.BlockSpec((SUBLANES, LANES), lambda i: (0, 0))],
        out_specs=[spec, spec],
        out_shape=[jax.ShapeDtypeStruct((t, LANES), F32)] * 2,
        compiler_params=_params("parallel"),
        name="rope_tables",
    )(pos_col, par)


def _partner_matrix(head, half):
    m = np.zeros((EP_CHUNK, EP_CHUNK), np.float32)
    for base in range(0, EP_CHUNK, head):
        for i in range(half):
            m[base + i + half, base + i] = 1.0
            m[base + i, base + i + half] = 1.0
    return jnp.asarray(m, MXU_DTYPE)


def _normed(x_ref, g_ref):
    xf = x_ref[...]
    ms = jnp.mean(xf * xf, axis=-1, keepdims=True)
    return (xf * lax.rsqrt(ms + NORM_EPS) * g_ref[...]).astype(MXU_DTYPE)


def _norm_proj_kernel(x_ref, g_ref, w_ref, o_ref, h_ref):
    @pl.when(pl.program_id(1) == 0)
    def _():
        h_ref[...] = _normed(x_ref, g_ref)

    o_ref[...] = jnp.dot(h_ref[...], w_ref[...], preferred_element_type=F32).astype(o_ref.dtype)


def _norm_proj(x2d, gain, w, *, tn, out_dtype, name):
    t, d = x2d.shape
    n = w.shape[1]
    tm = min(t, 1024)
    return pl.pallas_call(
        _norm_proj_kernel,
        grid=(t // tm, n // tn),
        in_specs=[pl.BlockSpec((tm, d), lambda i, j: (i, 0)),
                  pl.BlockSpec((1, d), lambda i, j: (0, 0)),
                  pl.BlockSpec((d, tn), lambda i, j: (0, j))],
        out_specs=pl.BlockSpec((tm, tn), lambda i, j: (i, j)),
        out_shape=jax.ShapeDtypeStruct((t, n), out_dtype),
        scratch_shapes=[pltpu.VMEM((tm, d), MXU_DTYPE)],
        compiler_params=_params("parallel", "arbitrary"),
        name=name,
    )(x2d, gain, w)


NORM_ROPE, ROPE, PLAIN = "norm_rope", "rope", "plain"


def _norm_proj_heads_kernel(dil, kinds, x_ref, g_ref, w_ref, part_ref, c_ref, s_ref, hg_ref,
                            o_ref, y_buf):
    h = _normed(x_ref, g_ref)
    per_chunk = EP_CHUNK // LANES
    cos2 = jnp.concatenate([c_ref[...]] * per_chunk, axis=1)
    sin2 = jnp.concatenate([s_ref[...]] * per_chunk, axis=1)
    n = w_ref.shape[1]
    for g0 in range(0, n, PROJ_GROUP):
        width = min(PROJ_GROUP, n - g0)
        yg = jnp.dot(h, w_ref[:, g0:g0 + width], preferred_element_type=F32)
        for c0 in range(0, width, EP_CHUNK):
            kind = kinds[(g0 + c0) // EP_CHUNK]
            y = yg[:, c0:c0 + EP_CHUNK]
            if kind == NORM_ROPE:
                heads = []
                for k in range(EP_CHUNK // HEAD_DIM):
                    yh = y[:, k * HEAD_DIM:(k + 1) * HEAD_DIM]
                    ms = jnp.mean(yh * yh, axis=-1, keepdims=True)
                    lo = g0 + c0 + k * HEAD_DIM
                    heads.append(yh * lax.rsqrt(ms + NORM_EPS) * hg_ref[:, lo:lo + HEAD_DIM])
                y = jnp.concatenate(heads, axis=1)
            if kind != PLAIN:
                partner = jnp.dot(y.astype(MXU_DTYPE), part_ref[...], preferred_element_type=F32)
                y = y * cos2 + partner * sin2
            for k in range(per_chunk):
                y_buf[(g0 + c0) // LANES + k] = y[:, k * LANES:(k + 1) * LANES]

    rows = y_buf.shape[1] // dil
    for r in range(dil):
        for k in range(y_buf.shape[0]):
            slab = y_buf[k] if dil == 1 else y_buf[k, pl.ds(r, rows, stride=dil), :]
            o_ref[r, :, k * LANES:(k + 1) * LANES] = slab.astype(o_ref.dtype)


def _norm_proj_heads(x2d, gain, w, partner, tabs, head_gain, kinds, *, dil, name):
    t, d = x2d.shape
    n = w.shape[1]
    tm = min(t, 512)
    assert len(kinds) == n // EP_CHUNK
    tab = pl.BlockSpec((tm, LANES), lambda i: (i, 0))
    return pl.pallas_call(
        functools.partial(_norm_proj_heads_kernel, dil, tuple(kinds)),
        grid=(t // tm,),
        in_specs=[pl.BlockSpec((tm, d), lambda i: (i, 0)),
                  pl.BlockSpec((1, d), lambda i: (0, 0)),
                  pl.BlockSpec((d, n), lambda i: (0, 0)),
                  pl.BlockSpec((EP_CHUNK, EP_CHUNK), lambda i: (0, 0)),
                  tab, tab,
                  pl.BlockSpec((1, n), lambda i: (0, 0))],
        out_specs=pl.BlockSpec((dil, tm // dil, n), lambda i: (0, i, 0)),
        out_shape=jax.ShapeDtypeStruct((dil, t // dil, n), MXU_DTYPE),
        scratch_shapes=[pltpu.VMEM((n // LANES, tm, LANES), F32)],
        compiler_params=_params("parallel"),
        name=name,
    )(x2d, gain, w, partner, tabs[0], tabs[1], head_gain)


def _dilated_kernel(span, dil, q_ref, kc_ref, kp_ref, vc_ref, vp_ref, o_ref, lse_ref):
    i = pl.program_id(1)
    qi = lax.broadcasted_iota(jnp.int32, (A_BLOCK, 2 * A_BLOCK), 0) + A_BLOCK
    kj = lax.broadcasted_iota(jnp.int32, (A_BLOCK, 2 * A_BLOCK), 1)
    dist = qi - kj
    min_key = jnp.where(i > 0, 0, A_BLOCK)
    bias = jnp.where((dist >= 0) & (dist <= span) & (kj >= min_key), 0.0, NEG_BIG)

    def residue(r, carry):
        rows = pl.ds(r, A_BLOCK, stride=dil) if dil > 1 else slice(None)
        for h in range(A_HEADS_PER_GROUP):
            hs = slice(h * HEAD_DIM, (h + 1) * HEAD_DIM)
            q = q_ref[r, :, hs]
            k2 = jnp.concatenate([kp_ref[r, :, hs], kc_ref[r, :, hs]], axis=0)
            v2 = jnp.concatenate([vp_ref[r, :, hs], vc_ref[r, :, hs]], axis=0)
            s = lax.dot_general(q, k2, (((1,), (1,)), ((), ())), preferred_element_type=F32) + bias
            m = jnp.max(s, axis=-1, keepdims=True)
            p = jnp.exp(s - m)
            l = jnp.sum(p, axis=-1, keepdims=True)
            o = jnp.dot(p.astype(v2.dtype), v2, preferred_element_type=F32)
            o_ref[h, rows, :] = o / l
            lse_ref[h, rows, :] = jnp.broadcast_to(m + jnp.log(l), (A_BLOCK, HEAD_DIM))
        return carry

    lax.fori_loop(0, dil, residue, 0)


def _dilated_group(qkv, bsz, s, window, dil):
    n = s // dil
    nb = n // A_BLOCK
    blk = (dil, A_BLOCK, A_GROUP_WIDTH)

    def prev(i):
        return jnp.maximum(i - 1, 0)

    in_specs = [
        pl.BlockSpec(blk, lambda b, i: (0, b * nb + i, 0)),
        pl.BlockSpec(blk, lambda b, i: (0, b * nb + i, 1)),
        pl.BlockSpec(blk, lambda b, i: (0, b * nb + prev(i), 1)),
        pl.BlockSpec(blk, lambda b, i: (0, b * nb + i, 2)),
        pl.BlockSpec(blk, lambda b, i: (0, b * nb + prev(i), 2)),
    ]
    out_spec = pl.BlockSpec((A_HEADS_PER_GROUP, dil * A_BLOCK, HEAD_DIM), lambda b, i: (0, b * nb + i, 0))
    out_sds = jax.ShapeDtypeStruct((A_HEADS_PER_GROUP, bsz * s, HEAD_DIM), F32)
    return pl.pallas_call(
        functools.partial(_dilated_kernel, window // dil, dil),
        grid=(bsz, nb),
        in_specs=in_specs,
        out_specs=[out_spec, out_spec],
        out_shape=[out_sds, out_sds],
        compiler_params=_params("parallel", "arbitrary"),
        name=f"dilated_attn_d{dil}",
    )(qkv, qkv, qkv, qkv, qkv)


def _merge_groups_kernel(o0, o1, o2, l0, l1, l2, out_ref):
    for h in range(A_HEADS_PER_GROUP):
        a, b, c = l0[h], l1[h], l2[h]
        m = jnp.maximum(jnp.maximum(a, b), c)
        ea, eb, ec = jnp.exp(a - m), jnp.exp(b - m), jnp.exp(c - m)
        num = ea * o0[h] + eb * o1[h] + ec * o2[h]
        out_ref[:, h * HEAD_DIM:(h + 1) * HEAD_DIM] = (num / (ea + eb + ec)).astype(out_ref.dtype)


def _merge_groups(os_, lses):
    t = os_[0].shape[1]
    tm = min(t, 1024)
    spec = pl.BlockSpec((A_HEADS_PER_GROUP, tm, HEAD_DIM), lambda i: (0, i, 0))
    return pl.pallas_call(
        _merge_groups_kernel,
        grid=(t // tm,),
        in_specs=[spec] * 6,
        out_specs=pl.BlockSpec((tm, A_GROUP_WIDTH), lambda i: (i, 0)),
        out_shape=jax.ShapeDtypeStruct((t, A_GROUP_WIDTH), MXU_DTYPE),
        compiler_params=_params("parallel"),
        name="dilated_merge",
    )(*os_, *lses)


def _tree_row_sum(x, group):
    parts = [x[g * group:(g + 1) * group, :] for g in range(x.shape[0] // group)]
    while len(parts) > 1:
        parts = [a + b for a, b in zip(parts[0::2], parts[1::2])]
    return parts[0]


def _sparse_kernel(topk, idx_scale, qit_ref, wt_ref, ki_ref, qbt_ref, kb_ref, vt_ref,
                   out_ref, keys_ref, hi_ref, acc_ref, m_ref, s_idx0, s_idx1, s_att0, s_att1,
                   p0, p1, al0, al1, bias_buf):
    qb = pl.program_id(1)
    t0 = qb * QB_BLOCK
    n_pairs = (t0 + QB_BLOCK + 2 * KEY_TILE - 1) // (2 * KEY_TILE)
    n_tiles = 2 * n_pairs
    last_tile = n_tiles - 1
    lane_q = lax.broadcasted_iota(jnp.int32, (KEY_TILE, QB_BLOCK), 1) + t0
    row_id = lax.broadcasted_iota(jnp.int32, (KEY_TILE, QB_BLOCK), 0)
    query_pos = lax.broadcasted_iota(jnp.int32, (1, QB_BLOCK), 1) + t0
    few_keys = query_pos < topk
    wts = wt_ref[...] * idx_scale

    def tile_start(kt):
        return pl.multiple_of(kt * KEY_TILE, KEY_TILE)

    def run_pipelined(steps):
        def body(j, carry):
            steps(j * STEPS_PER_BODY, STEPS_PER_BODY)
            return carry

        n_bodies = n_tiles // STEPS_PER_BODY
        lax.fori_loop(0, n_bodies, body, 0)

        @pl.when(n_tiles % STEPS_PER_BODY != 0)
        def _():
            steps(n_bodies * STEPS_PER_BODY, 2)

    def idx_matmul(kt, buf):
        k_tile = ki_ref[pl.ds(tile_start(kt), KEY_TILE), 0:IDX_DIM]
        buf[...] = jnp.dot(k_tile, qit_ref[...], preferred_element_type=F32)

    def idx_reduce(kt, buf):
        r0 = tile_start(kt)
        acc = jnp.zeros((KEY_TILE, QB_BLOCK), F32)
        for h in range(IDX_HEADS):
            acc = acc + jnp.maximum(buf[:, h * QB_BLOCK:(h + 1) * QB_BLOCK], 0.0) * wts[h:h + 1, :]
        acc = jnp.where(jnp.abs(acc) < F32_MIN_NORMAL, 0.0, acc)
        acc = jnp.where(row_id + r0 <= lane_q, acc, -jnp.inf)
        bits = pltpu.bitcast(acc, jnp.int32)
        keys_ref[pl.ds(r0, KEY_TILE), :] = bits ^ ((bits >> 31) & 0x7FFFFFFF)
        hi_ref[pl.ds(r0, KEY_TILE), :] = pltpu.bitcast(bits & HI16_MASK, F32).astype(jnp.bfloat16)

    def score_steps(first, count):
        bufs = (s_idx0, s_idx1)
        for k in range(count):
            idx_matmul(jnp.minimum(first + k + 1, last_tile), bufs[(k + 1) % 2])
            idx_reduce(first + k, bufs[k % 2])

    idx_matmul(0, s_idx0)
    run_pipelined(score_steps)

    def sum_over_keys(partial, acc_rows, dtype):
        per_chunk = COUNT_CHUNK // KEY_TILE
        n_chunks = n_tiles // per_chunk

        def chunk(c, acc):
            return acc + partial(pl.multiple_of(c * COUNT_CHUNK, COUNT_CHUNK), COUNT_CHUNK)

        def tile(kt, acc):
            return acc + partial(tile_start(kt), KEY_TILE)

        acc = lax.fori_loop(0, n_chunks, chunk, jnp.zeros((acc_rows, QB_BLOCK), dtype))
        acc = lax.fori_loop(n_chunks * per_chunk, n_tiles, tile, acc)
        return jnp.sum(acc, axis=0, keepdims=True)

    def count_prefix_ge(prefix):
        prefix = jnp.where((prefix > 0) & (prefix < BF16_MIN_NORMAL), BF16_MIN_NORMAL, prefix)
        fbits = (prefix ^ ((prefix >> 15) & 0x7FFF)) << 16
        cand = pltpu.bitcast(fbits, F32).astype(jnp.bfloat16)

        def partial(r0, rows):
            hit = jnp.where(hi_ref[pl.ds(r0, rows), :] >= cand,
                            jnp.ones((rows, QB_BLOCK), jnp.bfloat16),
                            jnp.zeros((rows, QB_BLOCK), jnp.bfloat16))
            return _tree_row_sum(hit, BF16_ROWS).astype(F32)

        return sum_over_keys(partial, BF16_ROWS, F32)

    want = jnp.float32(topk)
    zero_prefix = jnp.zeros((1, QB_BLOCK), jnp.int32)
    cnt0 = count_prefix_ge(zero_prefix)
    take0 = cnt0 >= want
    prefix = jnp.where(take0, zero_prefix, -(2 ** 15))
    cnt = jnp.where(take0, cnt0, jnp.float32(2 ** 24))

    def prefix_step(i, carry):
        prefix, cnt = carry
        cand = prefix | lax.shift_left(jnp.int32(1), 14 - i)
        c = count_prefix_ge(cand)
        take = c >= want
        return jnp.where(take, cand, prefix), jnp.where(take, c, cnt)

    prefix, cnt = lax.fori_loop(0, 15, prefix_step, (prefix, cnt))

    def count_where(pred):
        def partial(r0, rows):
            rid = lax.broadcasted_iota(jnp.int32, (rows, QB_BLOCK), 0) + r0
            hit = jnp.where(pred(keys_ref[pl.ds(r0, rows), :], rid), 1, 0)
            return _tree_row_sum(hit, SUBLANES)
        return sum_over_keys(partial, SUBLANES, jnp.int32)

    def count_ge(cand):
        return count_where(lambda k, rid: k >= cand)

    def unresolved(cnt):
        return jnp.max(jnp.where(few_keys | (cnt == topk), 0, 1)) > 0

    def low_cond(carry):
        bit, _, cnt = carry
        return (bit >= 0) & unresolved(cnt)

    def low_step(carry):
        bit, tau, cnt = carry
        cand = tau | lax.shift_left(jnp.int32(1), bit)
        c = count_ge(cand)
        take = c >= topk
        return bit - 1, jnp.where(take, cand, tau), jnp.where(take, c, cnt)

    _, tau, n_ge = lax.while_loop(low_cond, low_step,
                                  (jnp.int32(15), prefix << 16, cnt.astype(jnp.int32)))
    tau = jnp.where(few_keys, KEY_NEG_INF + 1, tau)

    @pl.when(jnp.max(jnp.where(few_keys, 0, n_ge)) > topk)
    def _():
        need = topk - count_ge(tau + 1)

        def idx_step(i, last):
            cand = last | lax.shift_left(jnp.int32(1), 30 - i)
            below = count_where(lambda k, rid: (k == tau) & (rid < cand))
            return jnp.where(below < need, cand, last)

        last = lax.fori_loop(0, 31, idx_step, jnp.zeros((1, QB_BLOCK), jnp.int32))

        def demote(kt, carry):
            r0 = tile_start(kt)
            k = keys_ref[pl.ds(r0, KEY_TILE), :]
            drop = (k == tau) & (row_id + r0 > last)
            keys_ref[pl.ds(r0, KEY_TILE), :] = jnp.where(drop, tau - 1, k)
            return carry

        lax.fori_loop(0, n_tiles, demote, 0)

    def att_matmul(kt, s_buf):
        s = jnp.dot(kb_ref[pl.ds(tile_start(kt), KEY_TILE), :], qbt_ref[...],
                    preferred_element_type=F32)
        s_buf[...] = s.astype(s_buf.dtype)

    def softmax_tile(kt, s_buf, p_buf, al_buf):
        sel = keys_ref[pl.ds(tile_start(kt), KEY_TILE), :] >= tau
        bias_buf[...] = jnp.where(sel, 0.0, NEG_BIG).astype(bias_buf.dtype)
        for h in range(B_HEADS):
            hs = slice(h * QB_BLOCK, (h + 1) * QB_BLOCK)
            s = s_buf[:, hs] + bias_buf[...]
            m_old = m_ref[:, hs]
            m_new = jnp.maximum(m_old, jnp.max(s, axis=0, keepdims=True).astype(F32))
            m_ref[:, hs] = m_new
            al_buf[:, hs] = jnp.exp2(m_old - m_new)
            p_buf[:, hs] = jnp.exp2(s - m_new.astype(s.dtype)).astype(p_buf.dtype)

    def value_matmul(kt, p_buf, al_buf):
        pv = jnp.dot(vt_ref[:, pl.ds(tile_start(kt), KEY_TILE)], p_buf[...],
                     preferred_element_type=F32)
        acc_ref[...] = acc_ref[...] * al_buf[...] + pv

    m_ref[...] = jnp.full(m_ref.shape, NEG_BIG, F32)
    acc_ref[...] = jnp.zeros(acc_ref.shape, F32)
    p1[...] = jnp.zeros(p1.shape, p1.dtype)
    al1[...] = jnp.ones(al1.shape, F32)
    att_matmul(0, s_att0)

    def attend_steps(first, count):
        s_bufs, p_bufs, al_bufs = (s_att0, s_att1), (p0, p1), (al0, al1)
        for k in range(count):
            cur, oth = k % 2, (k + 1) % 2
            att_matmul(jnp.minimum(first + k + 1, last_tile), s_bufs[oth])
            value_matmul(jnp.maximum(first + k - 1, 0), p_bufs[oth], al_bufs[oth])
            softmax_tile(first + k, s_bufs[cur], p_bufs[cur], al_bufs[cur])

    run_pipelined(attend_steps)
    value_matmul(last_tile, p1, al1)

    for h in range(B_HEADS):
        hs = slice(h * QB_BLOCK, (h + 1) * QB_BLOCK)
        o_t = acc_ref[0:HEAD_DIM, hs] / acc_ref[HEAD_DIM:HEAD_DIM + 1, hs]
        out_ref[:, h * HEAD_DIM:(h + 1) * HEAD_DIM] = o_t.T.astype(out_ref.dtype)


def _sparse_mixer(bproj, idx, wi_t, bsz, s):
    nqb = s // QB_BLOCK
    topk = min(B_TOPK_MAX, s // 4)
    idx_scale = (IDX_HEADS ** -0.5) * (IDX_DIM ** -0.5)
    nq = B_HEADS * QB_BLOCK
    qi = idx[:, :IDX_HEADS * IDX_DIM].reshape(bsz, nqb, QB_BLOCK, IDX_HEADS, IDX_DIM)
    qit = qi.transpose(0, 1, 4, 3, 2).reshape(bsz, nqb, IDX_DIM, IDX_HEADS * QB_BLOCK)
    qbq = bproj[:, :B_HEADS * HEAD_DIM].reshape(bsz, nqb, QB_BLOCK, B_HEADS, HEAD_DIM)
    qbt = qbq.transpose(0, 1, 4, 3, 2).reshape(bsz, nqb, HEAD_DIM, nq)
    vb = bproj[:, VB_COL:VB_COL + HEAD_DIM].reshape(bsz, s, HEAD_DIM)
    vt = jnp.concatenate([vb.transpose(0, 2, 1),
                          jnp.ones((bsz, 1, s), vb.dtype),
                          jnp.zeros((bsz, SUBLANES - 1, s), vb.dtype)], axis=1)
    v_rows = HEAD_DIM + SUBLANES
    b3 = bproj.reshape(bsz, s, B_COLS)
    idx3 = idx.reshape(bsz, s, IDX_COLS)
    kb_blk = KB_COL // LANES
    ki_blk = (IDX_HEADS * IDX_DIM) // LANES
    out = pl.pallas_call(
        functools.partial(_sparse_kernel, topk, idx_scale),
        grid=(bsz, nqb),
        in_specs=[
            pl.BlockSpec((None, None, IDX_DIM, IDX_HEADS * QB_BLOCK), lambda b, q: (b, q, 0, 0)),
            pl.BlockSpec((None, None, IDX_HEADS, QB_BLOCK), lambda b, q: (b, q, 0, 0)),
            pl.BlockSpec((None, s, LANES), lambda b, q: (b, 0, ki_blk)),
            pl.BlockSpec((None, None, HEAD_DIM, nq), lambda b, q: (b, q, 0, 0)),
            pl.BlockSpec((None, s, LANES), lambda b, q: (b, 0, kb_blk)),
            pl.BlockSpec((None, v_rows, s), lambda b, q: (b, 0, 0)),
        ],
        out_specs=pl.BlockSpec((None, QB_BLOCK, B_HEADS * HEAD_DIM), lambda b, q: (b, q, 0)),
        out_shape=jax.ShapeDtypeStruct((bsz, s, B_HEADS * HEAD_DIM), MXU_DTYPE),
        scratch_shapes=[pltpu.VMEM((s, QB_BLOCK), jnp.int32),
                        pltpu.VMEM((s, QB_BLOCK), jnp.bfloat16),
                        pltpu.VMEM((v_rows, nq), F32),
                        pltpu.VMEM((1, nq), F32),
                        pltpu.VMEM((KEY_TILE, IDX_HEADS * QB_BLOCK), F32),
                        pltpu.VMEM((KEY_TILE, IDX_HEADS * QB_BLOCK), F32),
                        pltpu.VMEM((KEY_TILE, nq), SOFTMAX_DTYPE),
                        pltpu.VMEM((KEY_TILE, nq), SOFTMAX_DTYPE),
                        pltpu.VMEM((KEY_TILE, nq), MXU_DTYPE),
                        pltpu.VMEM((KEY_TILE, nq), MXU_DTYPE),
                        pltpu.VMEM((1, nq), F32),
                        pltpu.VMEM((1, nq), F32),
                        pltpu.VMEM((KEY_TILE, QB_BLOCK), SOFTMAX_DTYPE)],
        compiler_params=_params("parallel", "arbitrary"),
        name="sparse_mixer",
    )(qit, wi_t, idx3, qbt, b3, vt)
    return out.reshape(bsz * s, B_HEADS * HEAD_DIM)


def _lru_kernel(ts, xc_ref, yc_ref, cw_ref, cb_ref, wa_ref, wx_ref, ba_ref, bx_ref, lam_ref,
                o_ref, xbuf, a_buf, b_buf, h_state):
    halo = SUBLANES

    @pl.when(pl.program_id(1) == 0)
    def _():
        xbuf[0:halo, :] = jnp.zeros((halo, xbuf.shape[1]), F32)
        h_state[...] = jnp.zeros(h_state.shape, F32)

    xbuf[halo:halo + ts, :] = xc_ref[...]
    xc = cb_ref[...] + cw_ref[0:1, :] * xbuf[halo - 3:halo - 3 + ts, :]
    for tap in range(1, LRU_CONV_W):
        lo = halo - (LRU_CONV_W - 1) + tap
        xc = xc + cw_ref[tap:tap + 1, :] * xbuf[lo:lo + ts, :]
    xbuf[0:halo, :] = xc_ref[ts - halo:ts, :]

    z = -lam_ref[...]
    softplus = jnp.maximum(z, 0.0) + jnp.log1p(jnp.exp(-jnp.abs(z)))
    bw = xc.shape[1] // LRU_BLOCKS
    for blk in range(LRU_BLOCKS):
        cs = slice(blk * bw, (blk + 1) * bw)
        xs = xc[:, cs]
        xs_m = xs.astype(MXU_DTYPE)
        r = _sigmoid(jnp.dot(xs_m, wa_ref[blk], preferred_element_type=F32) + ba_ref[:, cs])
        gate_i = _sigmoid(jnp.dot(xs_m, wx_ref[blk], preferred_element_type=F32) + bx_ref[:, cs])
        log_a = (-LRU_C) * r * softplus[:, cs]
        a = jnp.exp(log_a)
        a_buf[:, cs] = a
        b_buf[:, cs] = jnp.sqrt(-jnp.tanh(log_a) * (a * a + 1.0)) * gate_i * xs

    def step(t, h):
        h = a_buf[pl.ds(t, 1), :] * h + b_buf[pl.ds(t, 1), :]
        b_buf[pl.ds(t, 1), :] = h
        return h

    h_state[0:1, :] = lax.fori_loop(0, ts, step, h_state[0:1, :], unroll=8)
    o_ref[...] = (b_buf[...] * _gelu_tanh(yc_ref[...])).astype(o_ref.dtype)


def _lru_mixer(feat, bsz, s, width, conv_w, conv_b, wa, ba, wx, bx, lam):
    ts = min(s, 512)
    nt = s // ts
    row = lambda a: a.reshape(1, width).astype(F32)
    vec = pl.BlockSpec((1, width), lambda b, t: (0, 0))
    wspec = pl.BlockSpec(wa.shape, lambda b, t: (0, 0, 0))
    return pl.pallas_call(
        functools.partial(_lru_kernel, ts),
        grid=(bsz, nt),
        in_specs=[pl.BlockSpec((ts, width), lambda b, t: (b * nt + t, 0)),
                  pl.BlockSpec((ts, width), lambda b, t: (b * nt + t, 1)),
                  pl.BlockSpec((LRU_CONV_W, width), lambda b, t: (0, 0)),
                  vec, wspec, wspec, vec, vec, vec],
        out_specs=pl.BlockSpec((ts, width), lambda b, t: (b * nt + t, 0)),
        out_shape=jax.ShapeDtypeStruct((bsz * s, width), MXU_DTYPE),
        scratch_shapes=[pltpu.VMEM((ts + SUBLANES, width), F32),
                        pltpu.VMEM((ts, width), F32),
                        pltpu.VMEM((ts, width), F32),
                        pltpu.VMEM((SUBLANES, width), F32)],
        compiler_params=_params("parallel", "arbitrary"),
        name="rglru_mixer",
    )(feat, feat, conv_w.astype(F32), row(conv_b), wa.astype(MXU_DTYPE), wx.astype(MXU_DTYPE),
      row(ba), row(bx), row(lam))


def _gate_merge_kernel(oa, ob, oc, wa, wb, wc, g0, g1, g2, b0, b1, b2, out_ref):
    acc = _sigmoid(g0[...] + b0[...]) * jnp.dot(oa[...], wa[...], preferred_element_type=F32)
    acc = acc + _sigmoid(g1[...] + b1[...]) * jnp.dot(ob[...], wb[...], preferred_element_type=F32)
    acc = acc + _sigmoid(g2[...] + b2[...]) * jnp.dot(oc[...], wc[...], preferred_element_type=F32)
    out_ref[...] = acc.astype(out_ref.dtype)


def _gate_merge(out_a, out_b, out_c, wpa, wpb, wpc, feat, gate_col0, b_gate_row, d_model):
    t = out_a.shape[0]
    tm = min(t, 1024)
    tn = 512
    nj = d_model // tn
    g_blk = gate_col0 // tn

    def act(a):
        return pl.BlockSpec((tm, a.shape[1]), lambda i, j: (i, 0))

    def wgt(w):
        return pl.BlockSpec((w.shape[0], tn), lambda i, j: (0, j))

    gates = [pl.BlockSpec((tm, tn), functools.partial(lambda br, i, j: (i, g_blk + br * nj + j), br))
             for br in range(N_BRANCH)]
    biases = [pl.BlockSpec((1, tn), functools.partial(lambda br, i, j: (0, br * nj + j), br))
              for br in range(N_BRANCH)]
    return pl.pallas_call(
        _gate_merge_kernel,
        grid=(t // tm, nj),
        in_specs=[act(out_a), act(out_b), act(out_c), wgt(wpa), wgt(wpb), wgt(wpc)] + gates + biases,
        out_specs=pl.BlockSpec((tm, tn), lambda i, j: (i, j)),
        out_shape=jax.ShapeDtypeStruct((t, d_model), MXU_DTYPE),
        compiler_params=_params("parallel", "arbitrary"),
        name="gate_merge",
    )(out_a, out_b, out_c, wpa, wpb, wpc, feat, feat, feat, b_gate_row, b_gate_row, b_gate_row)


def _matmul_residual_kernel(a_ref, w_ref, r_ref, o_ref):
    o_ref[...] = r_ref[...] + jnp.dot(a_ref[...], w_ref[...], preferred_element_type=F32)


def _matmul_residual(a, w, resid, name):
    t, k = a.shape
    n = w.shape[1]
    tm = min(t, 1024)
    tn = 512
    return pl.pallas_call(
        _matmul_residual_kernel,
        grid=(t // tm, n // tn),
        in_specs=[pl.BlockSpec((tm, k), lambda i, j: (i, 0)),
                  pl.BlockSpec((k, tn), lambda i, j: (0, j)),
                  pl.BlockSpec((tm, tn), lambda i, j: (i, j))],
        out_specs=pl.BlockSpec((tm, tn), lambda i, j: (i, j)),
        out_shape=jax.ShapeDtypeStruct((t, n), F32),
        compiler_params=_params("parallel", "arbitrary"),
        name=name,
    )(a, w, resid)


def _ffn_up_kernel(tm, tiles_per_seq, x_ref, xh_ref, g_ref, wa_ref, wb_ref, cwa_ref, cwb_ref,
                   cba_ref, cbb_ref, o_ref, h_ref, hh_ref, ua_buf, ub_buf):
    halo = SUBLANES

    def normed(ref):
        xf = ref[...]
        ms = jnp.mean(xf * xf, axis=-1, keepdims=True)
        return (xf * lax.rsqrt(ms + NORM_EPS) * g_ref[...]).astype(MXU_DTYPE)

    @pl.when(pl.program_id(1) == 0)
    def _():
        h_ref[...] = normed(x_ref)
        hh_ref[...] = normed(xh_ref)

    keep = jnp.where(pl.program_id(0) % tiles_per_seq == 0, 0.0, 1.0)

    def conv_half(w_ref, cw_ref, cb_ref, buf):
        buf[0:halo, :] = keep * jnp.dot(hh_ref[...], w_ref[...], preferred_element_type=F32)
        buf[halo:halo + tm, :] = jnp.dot(h_ref[...], w_ref[...], preferred_element_type=F32)
        y = cb_ref[...] + cw_ref[0:1, :] * buf[halo - 2:halo - 2 + tm, :]
        for tap in range(1, FFN_CONV_W):
            lo = halo - (FFN_CONV_W - 1) + tap
            y = y + cw_ref[tap:tap + 1, :] * buf[lo:lo + tm, :]
        return y

    ya = conv_half(wa_ref, cwa_ref, cba_ref, ua_buf)
    yb = conv_half(wb_ref, cwb_ref, cbb_ref, ub_buf)
    o_ref[...] = (_gelu_tanh(ya) * yb).astype(o_ref.dtype)


def _ffn_up(x2d, s, gain, w_up, conv_w, conv_b_row):
    t, d = x2d.shape
    d_ff = w_up.shape[1] // 2
    tm = min(s, 1024)
    tn = 512
    nj = d_ff // tn
    hb = tm // SUBLANES
    return pl.pallas_call(
        functools.partial(_ffn_up_kernel, tm, s // tm),
        grid=(t // tm, nj),
        in_specs=[pl.BlockSpec((tm, d), lambda i, j: (i, 0)),
                  pl.BlockSpec((SUBLANES, d), lambda i, j: (jnp.maximum(i * hb - 1, 0), 0)),
                  pl.BlockSpec((1, d), lambda i, j: (0, 0)),
                  pl.BlockSpec((d, tn), lambda i, j: (0, j)),
                  pl.BlockSpec((d, tn), lambda i, j: (0, nj + j)),
                  pl.BlockSpec((FFN_CONV_W, tn), lambda i, j: (0, j)),
                  pl.BlockSpec((FFN_CONV_W, tn), lambda i, j: (0, nj + j)),
                  pl.BlockSpec((1, tn), lambda i, j: (0, j)),
                  pl.BlockSpec((1, tn), lambda i, j: (0, nj + j))],
        out_specs=pl.BlockSpec((tm, tn), lambda i, j: (i, j)),
        out_shape=jax.ShapeDtypeStruct((t, d_ff), MXU_DTYPE),
        scratch_shapes=[pltpu.VMEM((tm, d), MXU_DTYPE),
                        pltpu.VMEM((SUBLANES, d), MXU_DTYPE),
                        pltpu.VMEM((tm + SUBLANES, tn), F32),
                        pltpu.VMEM((tm + SUBLANES, tn), F32)],
        compiler_params=_params("parallel", "arbitrary"),
        name="ffn_up_conv_gate",
    )(x2d, x2d, gain, w_up, w_up, conv_w, conv_w, conv_b_row, conv_b_row)


def _pad_cols(parts, total, dtype):
    width = sum(p.shape[1] for p in parts)
    rows = parts[0].shape[0]
    if total > width:
        parts = list(parts) + [jnp.zeros((rows, total - width), parts[0].dtype)]
    return jnp.concatenate(parts, axis=1).astype(dtype)


def _temporal_block(x2d, bsz, s, tabs_head, tabs_idx, w_in, b_gate, qn_a, kn_a, qn_b, kn_b,
                    lru_conv_w, lru_conv_b, lru_wa, lru_ba, lru_wx, lru_bx, lru_lambda,
                    w_proj_a, w_proj_b, w_proj_c, w_out, norm_mix):
    d = x2d.shape[1]
    a_w = A_HEADS * HEAD_DIM
    b_w = B_HEADS * HEAD_DIM
    sizes = (a_w, a_w, a_w, b_w, HEAD_DIM, HEAD_DIM, IDX_HEADS * IDX_DIM, IDX_DIM, IDX_HEADS,
             d, d, N_BRANCH * d)
    offs = np.concatenate([[0], np.cumsum(sizes)])
    col = lambda k: w_in[:, offs[k]:offs[k + 1]]
    w_qa, w_ka, w_va, w_qb, w_kb, w_vb, w_qi, w_ki, w_wi, w_xc, w_yc, w_g = [col(k) for k in range(12)]

    gain_row = norm_mix.reshape(1, d).astype(F32)
    att_scale = HEAD_DIM ** -0.5
    ones = lambda k: jnp.ones((k,), F32)
    part_head = _partner_matrix(HEAD_DIM, ROT_DIM // 2)
    part_idx = _partner_matrix(IDX_DIM, IDX_ROT // 2)
    gw = A_GROUP_WIDTH

    a_gain = jnp.concatenate([jnp.tile(qn_a.astype(F32) * att_scale, A_HEADS_PER_GROUP),
                              jnp.tile(kn_a.astype(F32), A_HEADS_PER_GROUP), ones(gw)]).reshape(1, 3 * gw)
    a_kinds = [NORM_ROPE] * 4 + [PLAIN] * 2
    os_, lses = [], []
    for g, (window, dil) in enumerate(DIL_GROUPS):
        gs = slice(g * gw, (g + 1) * gw)
        w_g3 = jnp.concatenate([w_qa[:, gs], w_ka[:, gs], w_va[:, gs]], axis=1).astype(MXU_DTYPE)
        qkv = _norm_proj_heads(x2d, gain_row, w_g3, part_head, tabs_head, a_gain, a_kinds,
                               dil=dil, name=f"proj_a_d{dil}")
        o, lse = _dilated_group(qkv, bsz, s, window, dil)
        os_.append(o)
        lses.append(lse)
    out_a = _merge_groups(os_, lses)

    b_gain = jnp.concatenate([jnp.tile(qn_b.astype(F32) * (att_scale * LOG2_E), B_HEADS),
                              kn_b.astype(F32), ones(B_COLS - b_w - HEAD_DIM)]).reshape(1, B_COLS)
    zeros = jnp.zeros((d, HEAD_DIM), w_in.dtype)
    w_b = jnp.concatenate([w_qb, w_kb, zeros, w_vb, zeros], axis=1).astype(MXU_DTYPE)
    b_kinds = [NORM_ROPE] * (b_w // EP_CHUNK + 1) + [PLAIN]
    bproj = _norm_proj_heads(x2d, gain_row, w_b, part_head, tabs_head, b_gain, b_kinds,
                             dil=1, name="proj_b").reshape(bsz * s, B_COLS)
    idx = _norm_proj_heads(x2d, gain_row, _pad_cols([w_qi, w_ki], IDX_COLS, MXU_DTYPE), part_idx,
                           tabs_idx, jnp.ones((1, IDX_COLS), F32), [ROPE] * (IDX_COLS // EP_CHUNK),
                           dil=1, name="proj_idx").reshape(bsz * s, IDX_COLS)
    gate_col0 = 2 * d
    wi_col0 = gate_col0 + N_BRANCH * d
    feat_cols = -(-(wi_col0 + IDX_HEADS) // 512) * 512
    feat = _norm_proj(x2d, gain_row, _pad_cols([w_xc, w_yc, w_g, w_wi], feat_cols, MXU_DTYPE),
                      tn=512, out_dtype=F32, name="proj_feat")
    wi = feat[:, wi_col0:wi_col0 + IDX_HEADS].reshape(bsz, s // QB_BLOCK, QB_BLOCK, IDX_HEADS)
    out_b = _sparse_mixer(bproj, idx, wi.transpose(0, 1, 3, 2), bsz, s)

    out_c = _lru_mixer(feat, bsz, s, d, lru_conv_w, lru_conv_b, lru_wa, lru_ba, lru_wx, lru_bx,
                       lru_lambda)

    merged = _gate_merge(out_a, out_b, out_c, w_proj_a.astype(MXU_DTYPE),
                         w_proj_b.astype(MXU_DTYPE), w_proj_c.astype(MXU_DTYPE), feat, gate_col0,
                         b_gate.reshape(1, N_BRANCH * d).astype(F32), d)
    return _matmul_residual(merged, w_out.astype(MXU_DTYPE), x2d, "out_proj_residual")


def _conv_ffn_block(x2d, s, norm_ffn, ffn_w_up, ffn_conv_w, ffn_conv_b, ffn_w_down):
    d = x2d.shape[1]
    act = _ffn_up(x2d, s, norm_ffn.reshape(1, d).astype(F32), ffn_w_up.astype(MXU_DTYPE),
                  ffn_conv_w.astype(F32), ffn_conv_b.reshape(1, -1).astype(F32))
    return _matmul_residual(act, ffn_w_down.astype(MXU_DTYPE), x2d, "ffn_down_residual")


def kernel(x, positions, w_in, b_gate, qn_a, kn_a, qn_b, kn_b, lru_conv_w, lru_conv_b, lru_wa,
           lru_ba, lru_wx, lru_bx, lru_lambda, w_proj_a, w_proj_b, w_proj_c, w_out, norm_mix,
           norm_ffn, ffn_w_up, ffn_conv_w, ffn_conv_b, ffn_w_down):
    bsz, s, d = x.shape
    assert s % (DIL_GROUPS[-1][1] * A_BLOCK) == 0 and s % (2 * KEY_TILE) == 0
    x2d = x.reshape(bsz * s, d)
    pos_col = positions.astype(F32).reshape(bsz * s, 1)
    tabs_head = _rope_tables(pos_col, ROT_DIM, HEAD_DIM)
    tabs_idx = _rope_tables(pos_col, IDX_ROT, IDX_DIM)
    for layer in range(w_in.shape[0]):
        x2d = _temporal_block(
            x2d, bsz, s, tabs_head, tabs_idx, w_in[layer], b_gate[layer], qn_a[layer],
            kn_a[layer], qn_b[layer], kn_b[layer], lru_conv_w[layer], lru_conv_b[layer],
            lru_wa[layer], lru_ba[layer], lru_wx[layer], lru_bx[layer], lru_lambda[layer],
            w_proj_a[layer], w_proj_b[layer], w_proj_c[layer], w_out[layer], norm_mix[layer])
        x2d = _conv_ffn_block(x2d, s, norm_ffn[layer], ffn_w_up[layer], ffn_conv_w[layer],
                              ffn_conv_b[layer], ffn_w_down[layer])
    return x2d.reshape(bsz, s, d)
```

```python
import functools
import math

import numpy as np
import jax
import jax.numpy as jnp
from jax import lax
from jax.experimental import pallas as pl
from jax.experimental.pallas import tpu as pltpu

F32 = jnp.float32
MXU_DTYPE = jnp.bfloat16
SOFTMAX_DTYPE = jnp.bfloat16

LANES = 128
SUBLANES = 8
VMEM_LIMIT_BYTES = 58 * 1024 * 1024

HEAD_DIM = 128
ROPE_THETA = 500000.0
ROT_DIM = HEAD_DIM // 4
NORM_EPS = 1e-6
DIL_GROUPS = ((128, 1), (512, 4), (2048, 16))
A_HEADS_PER_GROUP = 4
A_HEADS = A_HEADS_PER_GROUP * len(DIL_GROUPS)
A_BLOCK = 128
B_HEADS = 12
B_TOPK_MAX = 256
IDX_HEADS = 16
IDX_DIM = 64
IDX_ROT = IDX_DIM // 4
LRU_BLOCKS = 16
LRU_C = 8.0
LRU_CONV_W = 4
FFN_CONV_W = 3
N_BRANCH = 3
A_GROUP_WIDTH = A_HEADS_PER_GROUP * HEAD_DIM

B_COLS = 2048
KB_COL = 1536
VB_COL = 1792
IDX_COLS = 1280
EP_CHUNK = 256
PROJ_GROUP = 512
QB_BLOCK = 128
KEY_TILE = 256
STEPS_PER_BODY = 2
INT_MIN = -(2 ** 31)
KEY_NEG_INF = INT_MIN + 0x7FFFFF
HI16_MASK = -65536
BF16_ROWS = 16
COUNT_CHUNK = 1024
BF16_MIN_NORMAL = 0x80
F32_MIN_NORMAL = 1.1754943508222875e-38
NEG_BIG = -(2.0 ** 100)
LOG2_E = 1.4426950408889634


def _params(*semantics):
    return pltpu.CompilerParams(dimension_semantics=semantics,
                                vmem_limit_bytes=VMEM_LIMIT_BYTES)


def _gelu_tanh(x):
    return 0.5 * x * (1.0 + jnp.tanh(0.7978845608028654 * (x + 0.044715 * (x * x * x))))


def _sigmoid(x):
    return 1.0 / (1.0 + jnp.exp(-x))


def _rope_tables_kernel(pos_ref, par_ref, c_ref, s_ref):
    ang = pos_ref[...] * par_ref[0:1, :]
    c_ref[...] = jnp.where(par_ref[1:2, :] > 0.0, jnp.cos(ang), 1.0)
    s_ref[...] = jnp.sin(ang) * par_ref[2:3, :]


def _rope_tables(pos_col, rot_dim, period):
    t = pos_col.shape[0]
    half = rot_dim // 2
    freq = ROPE_THETA ** (-jnp.arange(half, dtype=F32) / half)
    lane = np.arange(LANES) % period
    in_rot = lane < rot_dim
    freq_lane = jnp.where(jnp.asarray(in_rot), freq[lane % half], 0.0)
    sign = np.where(lane < half, -1.0, np.where(in_rot, 1.0, 0.0))
    rows = [freq_lane, jnp.asarray(in_rot, F32), jnp.asarray(sign, F32)]
    par = jnp.stack(rows + [jnp.zeros((LANES,), F32)] * 5, axis=0)
    tm = min(t, 2048)
    spec = pl.BlockSpec((tm, LANES), lambda i: (i, 0))
    return pl.pallas_call(
        _rope_tables_kernel,
        grid=(t // tm,),
        in_specs=[pl.BlockSpec((tm, 1), lambda i: (i, 0)),
                  pl.BlockSpec((SUBLANES, LANES), lambda i: (0, 0))],
        out_specs=[spec, spec],
        out_shape=[jax.ShapeDtypeStruct((t, LANES), F32)] * 2,
        compiler_params=_params("parallel"),
        name="rope_tables",
    )(pos_col, par)


def _partner_matrix(head, half):
    m = np.zeros((EP_CHUNK, EP_CHUNK), np.float32)
    for base in range(0, EP_CHUNK, head):
        for i in range(half):
            m[base + i + half, base + i] = 1.0
            m[base + i, base + i + half] = 1.0
    return jnp.asarray(m, MXU_DTYPE)


def _normed(x_ref, g_ref):
    xf = x_ref[...]
    ms = jnp.mean(xf * xf, axis=-1, keepdims=True)
    return (xf * lax.rsqrt(ms + NORM_EPS) * g_ref[...]).astype(MXU_DTYPE)


def _norm_proj_kernel(x_ref, g_ref, w_ref, o_ref, h_ref):
    @pl.when(pl.program_id(1) == 0)
    def _():
        h_ref[...] = _normed(x_ref, g_ref)

    o_ref[...] = jnp.dot(h_ref[...], w_ref[...], preferred_element_type=F32).astype(o_ref.dtype)


def _norm_proj(x2d, gain, w, *, tn, out_dtype, name):
    t, d = x2d.shape
    n = w.shape[1]
    tm = min(t, 1024)
    return pl.pallas_call(
        _norm_proj_kernel,
        grid=(t // tm, n // tn),
        in_specs=[pl.BlockSpec((tm, d), lambda i, j: (i, 0)),
                  pl.BlockSpec((1, d), lambda i, j: (0, 0)),
                  pl.BlockSpec((d, tn), lambda i, j: (0, j))],
        out_specs=pl.BlockSpec((tm, tn), lambda i, j: (i, j)),
        out_shape=jax.ShapeDtypeStruct((t, n), out_dtype),
        scratch_shapes=[pltpu.VMEM((tm, d), MXU_DTYPE)],
        compiler_params=_params("parallel", "arbitrary"),
        name=name,
    )(x2d, gain, w)


NORM_ROPE, ROPE, PLAIN = "norm_rope", "rope", "plain"


def _norm_proj_heads_kernel(dil, kinds, x_ref, g_ref, w_ref, part_ref, c_ref, s_ref, hg_ref,
                            o_ref, y_buf):
    h = _normed(x_ref, g_ref)
    per_chunk = EP_CHUNK // LANES
    cos2 = jnp.concatenate([c_ref[...]] * per_chunk, axis=1)
    sin2 = jnp.concatenate([s_ref[...]] * per_chunk, axis=1)
    n = w_ref.shape[1]
    for g0 in range(0, n, PROJ_GROUP):
        width = min(PROJ_GROUP, n - g0)
        yg = jnp.dot(h, w_ref[:, g0:g0 + width], preferred_element_type=F32)
        for c0 in range(0, width, EP_CHUNK):
            kind = kinds[(g0 + c0) // EP_CHUNK]
            y = yg[:, c0:c0 + EP_CHUNK]
            if kind == NORM_ROPE:
                heads = []
                for k in range(EP_CHUNK // HEAD_DIM):
                    yh = y[:, k * HEAD_DIM:(k + 1) * HEAD_DIM]
                    ms = jnp.mean(yh * yh, axis=-1, keepdims=True)
                    lo = g0 + c0 + k * HEAD_DIM
                    heads.append(yh * lax.rsqrt(ms + NORM_EPS) * hg_ref[:, lo:lo + HEAD_DIM])
                y = jnp.concatenate(heads, axis=1)
            if kind != PLAIN:
                partner = jnp.dot(y.astype(MXU_DTYPE), part_ref[...], preferred_element_type=F32)
                y = y * cos2 + partner * sin2
            for k in range(per_chunk):
                y_buf[(g0 + c0) // LANES + k] = y[:, k * LANES:(k + 1) * LANES]

    rows = y_buf.shape[1] // dil
    for r in range(dil):
        for k in range(y_buf.shape[0]):
            slab = y_buf[k] if dil == 1 else y_buf[k, pl.ds(r, rows, stride=dil), :]
            o_ref[r, :, k * LANES:(k + 1) * LANES] = slab.astype(o_ref.dtype)


def _norm_proj_heads(x2d, gain, w, partner, tabs, head_gain, kinds, *, dil, name):
    t, d = x2d.shape
    n = w.shape[1]
    tm = min(t, 512)
    assert len(kinds) == n // EP_CHUNK
    tab = pl.BlockSpec((tm, LANES), lambda i: (i, 0))
    return pl.pallas_call(
        functools.partial(_norm_proj_heads_kernel, dil, tuple(kinds)),
        grid=(t // tm,),
        in_specs=[pl.BlockSpec((tm, d), lambda i: (i, 0)),
                  pl.BlockSpec((1, d), lambda i: (0, 0)),
                  pl.BlockSpec((d, n), lambda i: (0, 0)),
                  pl.BlockSpec((EP_CHUNK, EP_CHUNK), lambda i: (0, 0)),
                  tab, tab,
                  pl.BlockSpec((1, n), lambda i: (0, 0))],
        out_specs=pl.BlockSpec((dil, tm // dil, n), lambda i: (0, i, 0)),
        out_shape=jax.ShapeDtypeStruct((dil, t // dil, n), MXU_DTYPE),
        scratch_shapes=[pltpu.VMEM((n // LANES, tm, LANES), F32)],
        compiler_params=_params("parallel"),
        name=name,
    )(x2d, gain, w, partner, tabs[0], tabs[1], head_gain)


def _dilated_kernel(span, dil, q_ref, kc_ref, kp_ref, vc_ref, vp_ref, o_ref, lse_ref):
    i = pl.program_id(1)
    qi = lax.broadcasted_iota(jnp.int32, (A_BLOCK, 2 * A_BLOCK), 0) + A_BLOCK
    kj = lax.broadcasted_iota(jnp.int32, (A_BLOCK, 2 * A_BLOCK), 1)
    dist = qi - kj
    min_key = jnp.where(i > 0, 0, A_BLOCK)
    bias = jnp.where((dist >= 0) & (dist <= span) & (kj >= min_key), 0.0, NEG_BIG)

    def residue(r, carry):
        rows = pl.ds(r, A_BLOCK, stride=dil) if dil > 1 else slice(None)
        for h in range(A_HEADS_PER_GROUP):
            hs = slice(h * HEAD_DIM, (h + 1) * HEAD_DIM)
            q = q_ref[r, :, hs]
            k2 = jnp.concatenate([kp_ref[r, :, hs], kc_ref[r, :, hs]], axis=0)
            v2 = jnp.concatenate([vp_ref[r, :, hs], vc_ref[r, :, hs]], axis=0)
            s = lax.dot_general(q, k2, (((1,), (1,)), ((), ())), preferred_element_type=F32) + bias
            m = jnp.max(s, axis=-1, keepdims=True)
            p = jnp.exp(s - m)
            l = jnp.sum(p, axis=-1, keepdims=True)
            o = jnp.dot(p.astype(v2.dtype), v2, preferred_element_type=F32)
            o_ref[h, rows, :] = o / l
            lse_ref[h, rows, :] = jnp.broadcast_to(m + jnp.log(l), (A_BLOCK, HEAD_DIM))
        return carry

    lax.fori_loop(0, dil, residue, 0)


def _dilated_group(qkv, bsz, s, window, dil):
    n = s // dil
    nb = n // A_BLOCK
    blk = (dil, A_BLOCK, A_GROUP_WIDTH)

    def prev(i):
        return jnp.maximum(i - 1, 0)

    in_specs = [
        pl.BlockSpec(blk, lambda b, i: (0, b * nb + i, 0)),
        pl.BlockSpec(blk, lambda b, i: (0, b * nb + i, 1)),
        pl.BlockSpec(blk, lambda b, i: (0, b * nb + prev(i), 1)),
        pl.BlockSpec(blk, lambda b, i: (0, b * nb + i, 2)),
        pl.BlockSpec(blk, lambda b, i: (0, b * nb + prev(i), 2)),
    ]
    out_spec = pl.BlockSpec((A_HEADS_PER_GROUP, dil * A_BLOCK, HEAD_DIM), lambda b, i: (0, b * nb + i, 0))
    out_sds = jax.ShapeDtypeStruct((A_HEADS_PER_GROUP, bsz * s, HEAD_DIM), F32)
    return pl.pallas_call(
        functools.partial(_dilated_kernel, window // dil, dil),
        grid=(bsz, nb),
        in_specs=in_specs,
        out_specs=[out_spec, out_spec],
        out_shape=[out_sds, out_sds],
        compiler_params=_params("parallel", "arbitrary"),
        name=f"dilated_attn_d{dil}",
    )(qkv, qkv, qkv, qkv, qkv)


def _merge_groups_kernel(o0, o1, o2, l0, l1, l2, out_ref):
    for h in range(A_HEADS_PER_GROUP):
        a, b, c = l0[h], l1[h], l2[h]
        m = jnp.maximum(jnp.maximum(a, b), c)
        ea, eb, ec = jnp.exp(a - m), jnp.exp(b - m), jnp.exp(c - m)
        num = ea * o0[h] + eb * o1[h] + ec * o2[h]
        out_ref[:, h * HEAD_DIM:(h + 1) * HEAD_DIM] = (num / (ea + eb + ec)).astype(out_ref.dtype)


def _merge_groups(os_, lses):
    t = os_[0].shape[1]
    tm = min(t, 1024)
    spec = pl.BlockSpec((A_HEADS_PER_GROUP, tm, HEAD_DIM), lambda i: (0, i, 0))
    return pl.pallas_call(
        _merge_groups_kernel,
        grid=(t // tm,),
        in_specs=[spec] * 6,
        out_specs=pl.BlockSpec((tm, A_GROUP_WIDTH), lambda i: (i, 0)),
        out_shape=jax.ShapeDtypeStruct((t, A_GROUP_WIDTH), MXU_DTYPE),
        compiler_params=_params("parallel"),
        name="dilated_merge",
    )(*os_, *lses)


def _tree_row_sum(x, group):
    parts = [x[g * group:(g + 1) * group, :] for g in range(x.shape[0] // group)]
    while len(parts) > 1:
        parts = [a + b for a, b in zip(parts[0::2], parts[1::2])]
    return parts[0]


def _sparse_kernel(topk, idx_scale, qit_ref, wt_ref, ki_ref, qbt_ref, kb_ref, vt_ref,
                   out_ref, keys_ref, hi_ref, acc_ref, m_ref, s_idx0, s_idx1, s_att0, s_att1,
                   p0, p1, al0, al1, bias_buf):
    qb = pl.program_id(1)
    t0 = qb * QB_BLOCK
    n_pairs = (t0 + QB_BLOCK + 2 * KEY_TILE - 1) // (2 * KEY_TILE)
    n_tiles = 2 * n_pairs
    last_tile = n_tiles - 1
    lane_q = lax.broadcasted_iota(jnp.int32, (KEY_TILE, QB_BLOCK), 1) + t0
    row_id = lax.broadcasted_iota(jnp.int32, (KEY_TILE, QB_BLOCK), 0)
    query_pos = lax.broadcasted_iota(jnp.int32, (1, QB_BLOCK), 1) + t0
    few_keys = query_pos < topk
    wts = wt_ref[...] * idx_scale

    def tile_start(kt):
        return pl.multiple_of(kt * KEY_TILE, KEY_TILE)

    def run_pipelined(steps):
        def body(j, carry):
            steps(j * STEPS_PER_BODY, STEPS_PER_BODY)
            return carry

        n_bodies = n_tiles // STEPS_PER_BODY
        lax.fori_loop(0, n_bodies, body, 0)
        if STEPS_PER_BODY > 2:
            @pl.when(n_tiles % STEPS_PER_BODY != 0)
            def _():
                steps(n_bodies * STEPS_PER_BODY, 2)

    def idx_matmul(kt, buf):
        k_tile = ki_ref[pl.ds(tile_start(kt), KEY_TILE), 0:IDX_DIM]
        buf[...] = jnp.dot(k_tile, qit_ref[...], preferred_element_type=F32)

    def idx_reduce(kt, buf):
        r0 = tile_start(kt)
        acc = jnp.zeros((KEY_TILE, QB_BLOCK), F32)
        for h in range(IDX_HEADS):
            acc = acc + jnp.maximum(buf[:, h * QB_BLOCK:(h + 1) * QB_BLOCK], 0.0) * wts[h:h + 1, :]
        acc = jnp.where(jnp.abs(acc) < F32_MIN_NORMAL, 0.0, acc)
        acc = jnp.where(row_id + r0 <= lane_q, acc, -jnp.inf)
        bits = pltpu.bitcast(acc, jnp.int32)
        keys_ref[pl.ds(r0, KEY_TILE), :] = bits ^ ((bits >> 31) & 0x7FFFFFFF)
        hi_ref[pl.ds(r0, KEY_TILE), :] = pltpu.bitcast(bits & HI16_MASK, F32).astype(jnp.bfloat16)

    def score_steps(first, count):
        bufs = (s_idx0, s_idx1)
        for k in range(count):
            idx_matmul(jnp.minimum(first + k + 1, last_tile), bufs[(k + 1) % 2])
            idx_reduce(first + k, bufs[k % 2])

    idx_matmul(0, s_idx0)
    run_pipelined(score_steps)

    def sum_over_keys(partial, acc_rows, dtype):
        per_chunk = COUNT_CHUNK // KEY_TILE
        n_chunks = n_tiles // per_chunk

        def chunk(c, acc):
            return acc + partial(pl.multiple_of(c * COUNT_CHUNK, COUNT_CHUNK), COUNT_CHUNK)

        def tile(kt, acc):
            return acc + partial(tile_start(kt), KEY_TILE)

        acc = lax.fori_loop(0, n_chunks, chunk, jnp.zeros((acc_rows, QB_BLOCK), dtype))
        acc = lax.fori_loop(n_chunks * per_chunk, n_tiles, tile, acc)
        return jnp.sum(acc, axis=0, keepdims=True)

    def count_prefix_ge(prefix):
        prefix = jnp.where((prefix > 0) & (prefix < BF16_MIN_NORMAL), BF16_MIN_NORMAL, prefix)
        fbits = (prefix ^ ((prefix >> 15) & 0x7FFF)) << 16
        cand = pltpu.bitcast(fbits, F32).astype(jnp.bfloat16)

        def partial(r0, rows):
            hit = jnp.where(hi_ref[pl.ds(r0, rows), :] >= cand,
                            jnp.ones((rows, QB_BLOCK), jnp.bfloat16),
                            jnp.zeros((rows, QB_BLOCK), jnp.bfloat16))
            return _tree_row_sum(hit, BF16_ROWS).astype(F32)

        return sum_over_keys(partial, BF16_ROWS, F32)

    want = jnp.float32(topk)
    zero_prefix = jnp.zeros((1, QB_BLOCK), jnp.int32)
    cnt0 = count_prefix_ge(zero_prefix)
    take0 = cnt0 >= want
    prefix = jnp.where(take0, zero_prefix, -(2 ** 15))
    cnt = jnp.where(take0, cnt0, jnp.float32(2 ** 24))

    def prefix_step(i, carry):
        prefix, cnt = carry
        cand = prefix | lax.shift_left(jnp.int32(1), 14 - i)
        c = count_prefix_ge(cand)
        take = c >= want
        return jnp.where(take, cand, prefix), jnp.where(take, c, cnt)

    prefix, cnt = lax.fori_loop(0, 15, prefix_step, (prefix, cnt))

    def count_where(pred):
        def partial(r0, rows):
            rid = lax.broadcasted_iota(jnp.int32, (rows, QB_BLOCK), 0) + r0
            hit = jnp.where(pred(keys_ref[pl.ds(r0, rows), :], rid), 1, 0)
            return _tree_row_sum(hit, SUBLANES)
        return sum_over_keys(partial, SUBLANES, jnp.int32)

    def count_ge(cand):
        return count_where(lambda k, rid: k >= cand)

    def unresolved(cnt):
        return jnp.max(jnp.where(few_keys | (cnt == topk), 0, 1)) > 0

    def low_cond(carry):
        bit, _, cnt = carry
        return (bit >= 0) & unresolved(cnt)

    def low_step(carry):
        bit, tau, cnt = carry
        cand = tau | lax.shift_left(jnp.int32(1), bit)
        c = count_ge(cand)
        take = c >= topk
        return bit - 1, jnp.where(take, cand, tau), jnp.where(take, c, cnt)

    _, tau, n_ge = lax.while_loop(low_cond, low_step,
                                  (jnp.int32(15), prefix << 16, cnt.astype(jnp.int32)))
    tau = jnp.where(few_keys, KEY_NEG_INF + 1, tau)

    @pl.when(jnp.max(jnp.where(few_keys, 0, n_ge)) > topk)
    def _():
        need = topk - count_ge(tau + 1)

        def idx_step(i, last):
            cand = last | lax.shift_left(jnp.int32(1), 30 - i)
            below = count_where(lambda k, rid: (k == tau) & (rid < cand))
            return jnp.where(below < need, cand, last)

        last = lax.fori_loop(0, 31, idx_step, jnp.zeros((1, QB_BLOCK), jnp.int32))

        def demote(kt, carry):
            r0 = tile_start(kt)
            k = keys_ref[pl.ds(r0, KEY_TILE), :]
            drop = (k == tau) & (row_id + r0 > last)
            keys_ref[pl.ds(r0, KEY_TILE), :] = jnp.where(drop, tau - 1, k)
            return carry

        lax.fori_loop(0, n_tiles, demote, 0)

    def att_matmul(kt, s_buf):
        s = jnp.dot(kb_ref[pl.ds(tile_start(kt), KEY_TILE), :], qbt_ref[...],
                    preferred_element_type=F32)
        s_buf[...] = s.astype(s_buf.dtype)

    def softmax_tile(kt, s_buf, p_buf, al_buf):
        sel = keys_ref[pl.ds(tile_start(kt), KEY_TILE), :] >= tau
        bias_buf[...] = jnp.where(sel, 0.0, NEG_BIG).astype(bias_buf.dtype)
        for h in range(B_HEADS):
            hs = slice(h * QB_BLOCK, (h + 1) * QB_BLOCK)
            s = s_buf[:, hs] + bias_buf[...]
            m_old = m_ref[:, hs]
            m_new = jnp.maximum(m_old, jnp.max(s, axis=0, keepdims=True).astype(F32))
            m_ref[:, hs] = m_new
            al_buf[:, hs] = jnp.exp2(m_old - m_new)
            p_buf[:, hs] = jnp.exp2(s - m_new.astype(s.dtype)).astype(p_buf.dtype)

    def value_matmul(kt, p_buf, al_buf):
        pv = jnp.dot(vt_ref[:, pl.ds(tile_start(kt), KEY_TILE)], p_buf[...],
                     preferred_element_type=F32)
        acc_ref[...] = acc_ref[...] * al_buf[...] + pv

    m_ref[...] = jnp.full(m_ref.shape, NEG_BIG, F32)
    acc_ref[...] = jnp.zeros(acc_ref.shape, F32)
    p1[...] = jnp.zeros(p1.shape, p1.dtype)
    al1[...] = jnp.ones(al1.shape, F32)
    att_matmul(0, s_att0)

    def attend_steps(first, count):
        s_bufs, p_bufs, al_bufs = (s_att0, s_att1), (p0, p1), (al0, al1)
        for k in range(count):
            cur, oth = k % 2, (k + 1) % 2
            att_matmul(jnp.minimum(first + k + 1, last_tile), s_bufs[oth])
            value_matmul(jnp.maximum(first + k - 1, 0), p_bufs[oth], al_bufs[oth])
            softmax_tile(first + k, s_bufs[cur], p_bufs[cur], al_bufs[cur])

    run_pipelined(attend_steps)
    value_matmul(last_tile, p1, al1)

    for h in range(B_HEADS):
        hs = slice(h * QB_BLOCK, (h + 1) * QB_BLOCK)
        o_t = acc_ref[0:HEAD_DIM, hs] / acc_ref[HEAD_DIM:HEAD_DIM + 1, hs]
        out_ref[:, h * HEAD_DIM:(h + 1) * HEAD_DIM] = o_t.T.astype(out_ref.dtype)


def _sparse_mixer(bproj, idx, wi_t, bsz, s):
    nqb = s // QB_BLOCK
    topk = min(B_TOPK_MAX, s // 4)
    idx_scale = (IDX_HEADS ** -0.5) * (IDX_DIM ** -0.5)
    nq = B_HEADS * QB_BLOCK
    qi = idx[:, :IDX_HEADS * IDX_DIM].reshape(bsz, nqb, QB_BLOCK, IDX_HEADS, IDX_DIM)
    qit = qi.transpose(0, 1, 4, 3, 2).reshape(bsz, nqb, IDX_DIM, IDX_HEADS * QB_BLOCK)
    qbq = bproj[:, :B_HEADS * HEAD_DIM].reshape(bsz, nqb, QB_BLOCK, B_HEADS, HEAD_DIM)
    qbt = qbq.transpose(0, 1, 4, 3, 2).reshape(bsz, nqb, HEAD_DIM, nq)
    vb = bproj[:, VB_COL:VB_COL + HEAD_DIM].reshape(bsz, s, HEAD_DIM)
    vt = jnp.concatenate([vb.transpose(0, 2, 1),
                          jnp.ones((bsz, 1, s), vb.dtype),
                          jnp.zeros((bsz, SUBLANES - 1, s), vb.dtype)], axis=1)
    v_rows = HEAD_DIM + SUBLANES
    b3 = bproj.reshape(bsz, s, B_COLS)
    idx3 = idx.reshape(bsz, s, IDX_COLS)
    kb_blk = KB_COL // LANES
    ki_blk = (IDX_HEADS * IDX_DIM) // LANES
    out = pl.pallas_call(
        functools.partial(_sparse_kernel, topk, idx_scale),
        grid=(bsz, nqb),
        in_specs=[
            pl.BlockSpec((None, None, IDX_DIM, IDX_HEADS * QB_BLOCK), lambda b, q: (b, q, 0, 0)),
            pl.BlockSpec((None, None, IDX_HEADS, QB_BLOCK), lambda b, q: (b, q, 0, 0)),
            pl.BlockSpec((None, s, LANES), lambda b, q: (b, 0, ki_blk)),
            pl.BlockSpec((None, None, HEAD_DIM, nq), lambda b, q: (b, q, 0, 0)),
            pl.BlockSpec((None, s, LANES), lambda b, q: (b, 0, kb_blk)),
            pl.BlockSpec((None, v_rows, s), lambda b, q: (b, 0, 0)),
        ],
        out_specs=pl.BlockSpec((None, QB_BLOCK, B_HEADS * HEAD_DIM), lambda b, q: (b, q, 0)),
        out_shape=jax.ShapeDtypeStruct((bsz, s, B_HEADS * HEAD_DIM), MXU_DTYPE),
        scratch_shapes=[pltpu.VMEM((s, QB_BLOCK), jnp.int32),
                        pltpu.VMEM((s, QB_BLOCK), jnp.bfloat16),
                        pltpu.VMEM((v_rows, nq), F32),
                        pltpu.VMEM((1, nq), F32),
                        pltpu.VMEM((KEY_TILE, IDX_HEADS * QB_BLOCK), F32),
                        pltpu.VMEM((KEY_TILE, IDX_HEADS * QB_BLOCK), F32),
                        pltpu.VMEM((KEY_TILE, nq), SOFTMAX_DTYPE),
                        pltpu.VMEM((KEY_TILE, nq), SOFTMAX_DTYPE),
                        pltpu.VMEM((KEY_TILE, nq), MXU_DTYPE),
                        pltpu.VMEM((KEY_TILE, nq), MXU_DTYPE),
                        pltpu.VMEM((1, nq), F32),
                        pltpu.VMEM((1, nq), F32),
                        pltpu.VMEM((KEY_TILE, QB_BLOCK), SOFTMAX_DTYPE)],
        compiler_params=_params("parallel", "arbitrary"),
        name="sparse_mixer",
    )(qit, wi_t, idx3, qbt, b3, vt)
    return out.reshape(bsz * s, B_HEADS * HEAD_DIM)


def _lru_kernel(ts, xc_ref, yc_ref, cw_ref, cb_ref, wa_ref, wx_ref, ba_ref, bx_ref, lam_ref,
                o_ref, xbuf, a_buf, b_buf, h_state):
    halo = SUBLANES

    @pl.when(pl.program_id(1) == 0)
    def _():
        xbuf[0:halo, :] = jnp.zeros((halo, xbuf.shape[1]), F32)
        h_state[...] = jnp.zeros(h_state.shape, F32)

    xbuf[halo:halo + ts, :] = xc_ref[...]
    xc = cb_ref[...] + cw_ref[0:1, :] * xbuf[halo - 3:halo - 3 + ts, :]
    for tap in range(1, LRU_CONV_W):
        lo = halo - (LRU_CONV_W - 1) + tap
        xc = xc + cw_ref[tap:tap + 1, :] * xbuf[lo:lo + ts, :]
    xbuf[0:halo, :] = xc_ref[ts - halo:ts, :]

    z = -lam_ref[...]
    softplus = jnp.maximum(z, 0.0) + jnp.log1p(jnp.exp(-jnp.abs(z)))
    bw = xc.shape[1] // LRU_BLOCKS
    for blk in range(LRU_BLOCKS):
        cs = slice(blk * bw, (blk + 1) * bw)
        xs = xc[:, cs]
        xs_m = xs.astype(MXU_DTYPE)
        r = _sigmoid(jnp.dot(xs_m, wa_ref[blk], preferred_element_type=F32) + ba_ref[:, cs])
        gate_i = _sigmoid(jnp.dot(xs_m, wx_ref[blk], preferred_element_type=F32) + bx_ref[:, cs])
        log_a = (-LRU_C) * r * softplus[:, cs]
        a = jnp.exp(log_a)
        a_buf[:, cs] = a
        b_buf[:, cs] = jnp.sqrt(-jnp.tanh(log_a) * (a * a + 1.0)) * gate_i * xs

    def step(t, h):
        h = a_buf[pl.ds(t, 1), :] * h + b_buf[pl.ds(t, 1), :]
        b_buf[pl.ds(t, 1), :] = h
        return h

    h_state[0:1, :] = lax.fori_loop(0, ts, step, h_state[0:1, :], unroll=8)
    o_ref[...] = (b_buf[...] * _gelu_tanh(yc_ref[...])).astype(o_ref.dtype)


def _lru_mixer(feat, bsz, s, width, conv_w, conv_b, wa, ba, wx, bx, lam):
    ts = min(s, 512)
    nt = s // ts
    row = lambda a: a.reshape(1, width).astype(F32)
    vec = pl.BlockSpec((1, width), lambda b, t: (0, 0))
    wspec = pl.BlockSpec(wa.shape, lambda b, t: (0, 0, 0))
    return pl.pallas_call(
        functools.partial(_lru_kernel, ts),
        grid=(bsz, nt),
        in_specs=[pl.BlockSpec((ts, width), lambda b, t: (b * nt + t, 0)),
                  pl.BlockSpec((ts, width), lambda b, t: (b * nt + t, 1)),
                  pl.BlockSpec((LRU_CONV_W, width), lambda b, t: (0, 0)),
                  vec, wspec, wspec, vec, vec, vec],
        out_specs=pl.BlockSpec((ts, width), lambda b, t: (b * nt + t, 0)),
        out_shape=jax.ShapeDtypeStruct((bsz * s, width), MXU_DTYPE),
        scratch_shapes=[pltpu.VMEM((ts + SUBLANES, width), F32),
                        pltpu.VMEM((ts, width), F32),
                        pltpu.VMEM((ts, width), F32),
                        pltpu.VMEM((SUBLANES, width), F32)],
        compiler_params=_params("parallel", "arbitrary"),
        name="rglru_mixer",
    )(feat, feat, conv_w.astype(F32), row(conv_b), wa.astype(MXU_DTYPE), wx.astype(MXU_DTYPE),
      row(ba), row(bx), row(lam))


def _gate_merge_kernel(oa, ob, oc, wa, wb, wc, g0, g1, g2, b0, b1, b2, out_ref):
    acc = _sigmoid(g0[...] + b0[...]) * jnp.dot(oa[...], wa[...], preferred_element_type=F32)
    acc = acc + _sigmoid(g1[...] + b1[...]) * jnp.dot(ob[...], wb[...], preferred_element_type=F32)
    acc = acc + _sigmoid(g2[...] + b2[...]) * jnp.dot(oc[...], wc[...], preferred_element_type=F32)
    out_ref[...] = acc.astype(out_ref.dtype)


def _gate_merge(out_a, out_b, out_c, wpa, wpb, wpc, feat, gate_col0, b_gate_row, d_model):
    t = out_a.shape[0]
    tm = min(t, 1024)
    tn = 512
    nj = d_model // tn
    g_blk = gate_col0 // tn

    def act(a):
        return pl.BlockSpec((tm, a.shape[1]), lambda i, j: (i, 0))

    def wgt(w):
        return pl.BlockSpec((w.shape[0], tn), lambda i, j: (0, j))

    gates = [pl.BlockSpec((tm, tn), functools.partial(lambda br, i, j: (i, g_blk + br * nj + j), br))
             for br in range(N_BRANCH)]
    biases = [pl.BlockSpec((1, tn), functools.partial(lambda br, i, j: (0, br * nj + j), br))
              for br in range(N_BRANCH)]
    return pl.pallas_call(
        _gate_merge_kernel,
        grid=(t // tm, nj),
        in_specs=[act(out_a), act(out_b), act(out_c), wgt(wpa), wgt(wpb), wgt(wpc)] + gates + biases,
        out_specs=pl.BlockSpec((tm, tn), lambda i, j: (i, j)),
        out_shape=jax.ShapeDtypeStruct((t, d_model), MXU_DTYPE),
        compiler_params=_params("parallel", "arbitrary"),
        name="gate_merge",
    )(out_a, out_b, out_c, wpa, wpb, wpc, feat, feat, feat, b_gate_row, b_gate_row, b_gate_row)


def _matmul_residual_kernel(a_ref, w_ref, r_ref, o_ref):
    o_ref[...] = r_ref[...] + jnp.dot(a_ref[...], w_ref[...], preferred_element_type=F32)


def _matmul_residual(a, w, resid, name):
    t, k = a.shape
    n = w.shape[1]
    tm = min(t, 1024)
    tn = 512
    return pl.pallas_call(
        _matmul_residual_kernel,
        grid=(t // tm, n // tn),
        in_specs=[pl.BlockSpec((tm, k), lambda i, j: (i, 0)),
                  pl.BlockSpec((k, tn), lambda i, j: (0, j)),
                  pl.BlockSpec((tm, tn), lambda i, j: (i, j))],
        out_specs=pl.BlockSpec((tm, tn), lambda i, j: (i, j)),
        out_shape=jax.ShapeDtypeStruct((t, n), F32),
        compiler_params=_params("parallel", "arbitrary"),
        name=name,
    )(a, w, resid)


def _ffn_up_kernel(tm, tiles_per_seq, x_ref, xh_ref, g_ref, wa_ref, wb_ref, cwa_ref, cwb_ref,
                   cba_ref, cbb_ref, o_ref, h_ref, hh_ref, ua_buf, ub_buf):
    halo = SUBLANES

    def normed(ref):
        xf = ref[...]
        ms = jnp.mean(xf * xf, axis=-1, keepdims=True)
        return (xf * lax.rsqrt(ms + NORM_EPS) * g_ref[...]).astype(MXU_DTYPE)

    @pl.when(pl.program_id(1) == 0)
    def _():
        h_ref[...] = normed(x_ref)
        hh_ref[...] = normed(xh_ref)

    keep = jnp.where(pl.program_id(0) % tiles_per_seq == 0, 0.0, 1.0)

    def conv_half(w_ref, cw_ref, cb_ref, buf):
        buf[0:halo, :] = keep * jnp.dot(hh_ref[...], w_ref[...], preferred_element_type=F32)
        buf[halo:halo + tm, :] = jnp.dot(h_ref[...], w_ref[...], preferred_element_type=F32)
        y = cb_ref[...] + cw_ref[0:1, :] * buf[halo - 2:halo - 2 + tm, :]
        for tap in range(1, FFN_CONV_W):
            lo = halo - (FFN_CONV_W - 1) + tap
            y = y + cw_ref[tap:tap + 1, :] * buf[lo:lo + tm, :]
        return y

    ya = conv_half(wa_ref, cwa_ref, cba_ref, ua_buf)
    yb = conv_half(wb_ref, cwb_ref, cbb_ref, ub_buf)
    o_ref[...] = (_gelu_tanh(ya) * yb).astype(o_ref.dtype)


def _ffn_up(x2d, s, gain, w_up, conv_w, conv_b_row):
    t, d = x2d.shape
    d_ff = w_up.shape[1] // 2
    tm = min(s, 1024)
    tn = 512
    nj = d_ff // tn
    hb = tm // SUBLANES
    return pl.pallas_call(
        functools.partial(_ffn_up_kernel, tm, s // tm),
        grid=(t // tm, nj),
        in_specs=[pl.BlockSpec((tm, d), lambda i, j: (i, 0)),
                  pl.BlockSpec((SUBLANES, d), lambda i, j: (jnp.maximum(i * hb - 1, 0), 0)),
                  pl.BlockSpec((1, d), lambda i, j: (0, 0)),
                  pl.BlockSpec((d, tn), lambda i, j: (0, j)),
                  pl.BlockSpec((d, tn), lambda i, j: (0, nj + j)),
                  pl.BlockSpec((FFN_CONV_W, tn), lambda i, j: (0, j)),
                  pl.BlockSpec((FFN_CONV_W, tn), lambda i, j: (0, nj + j)),
                  pl.BlockSpec((1, tn), lambda i, j: (0, j)),
                  pl.BlockSpec((1, tn), lambda i, j: (0, nj + j))],
        out_specs=pl.BlockSpec((tm, tn), lambda i, j: (i, j)),
        out_shape=jax.ShapeDtypeStruct((t, d_ff), MXU_DTYPE),
        scratch_shapes=[pltpu.VMEM((tm, d), MXU_DTYPE),
                        pltpu.VMEM((SUBLANES, d), MXU_DTYPE),
                        pltpu.VMEM((tm + SUBLANES, tn), F32),
                        pltpu.VMEM((tm + SUBLANES, tn), F32)],
        compiler_params=_params("parallel", "arbitrary"),
        name="ffn_up_conv_gate",
    )(x2d, x2d, gain, w_up, w_up, conv_w, conv_w, conv_b_row, conv_b_row)


def _pad_cols(parts, total, dtype):
    width = sum(p.shape[1] for p in parts)
    rows = parts[0].shape[0]
    if total > width:
        parts = list(parts) + [jnp.zeros((rows, total - width), parts[0].dtype)]
    return jnp.concatenate(parts, axis=1).astype(dtype)


def _temporal_block(x2d, bsz, s, tabs_head, tabs_idx, w_in, b_gate, qn_a, kn_a, qn_b, kn_b,
                    lru_conv_w, lru_conv_b, lru_wa, lru_ba, lru_wx, lru_bx, lru_lambda,
                    w_proj_a, w_proj_b, w_proj_c, w_out, norm_mix):
    d = x2d.shape[1]
    a_w = A_HEADS * HEAD_DIM
    b_w = B_HEADS * HEAD_DIM
    sizes = (a_w, a_w, a_w, b_w, HEAD_DIM, HEAD_DIM, IDX_HEADS * IDX_DIM, IDX_DIM, IDX_HEADS,
             d, d, N_BRANCH * d)
    offs = np.concatenate([[0], np.cumsum(sizes)])
    col = lambda k: w_in[:, offs[k]:offs[k + 1]]
    w_qa, w_ka, w_va, w_qb, w_kb, w_vb, w_qi, w_ki, w_wi, w_xc, w_yc, w_g = [col(k) for k in range(12)]

    gain_row = norm_mix.reshape(1, d).astype(F32)
    att_scale = HEAD_DIM ** -0.5
    ones = lambda k: jnp.ones((k,), F32)
    part_head = _partner_matrix(HEAD_DIM, ROT_DIM // 2)
    part_idx = _partner_matrix(IDX_DIM, IDX_ROT // 2)
    gw = A_GROUP_WIDTH

    a_gain = jnp.concatenate([jnp.tile(qn_a.astype(F32) * att_scale, A_HEADS_PER_GROUP),
                              jnp.tile(kn_a.astype(F32), A_HEADS_PER_GROUP), ones(gw)]).reshape(1, 3 * gw)
    a_kinds = [NORM_ROPE] * 4 + [PLAIN] * 2
    os_, lses = [], []
    for g, (window, dil) in enumerate(DIL_GROUPS):
        gs = slice(g * gw, (g + 1) * gw)
        w_g3 = jnp.concatenate([w_qa[:, gs], w_ka[:, gs], w_va[:, gs]], axis=1).astype(MXU_DTYPE)
        qkv = _norm_proj_heads(x2d, gain_row, w_g3, part_head, tabs_head, a_gain, a_kinds,
                               dil=dil, name=f"proj_a_d{dil}")
        o, lse = _dilated_group(qkv, bsz, s, window, dil)
        os_.append(o)
        lses.append(lse)
    out_a = _merge_groups(os_, lses)

    b_gain = jnp.concatenate([jnp.tile(qn_b.astype(F32) * (att_scale * LOG2_E), B_HEADS),
                              kn_b.astype(F32), ones(B_COLS - b_w - HEAD_DIM)]).reshape(1, B_COLS)
    zeros = jnp.zeros((d, HEAD_DIM), w_in.dtype)
    w_b = jnp.concatenate([w_qb, w_kb, zeros, w_vb, zeros], axis=1).astype(MXU_DTYPE)
    b_kinds = [NORM_ROPE] * (b_w // EP_CHUNK + 1) + [PLAIN]
    bproj = _norm_proj_heads(x2d, gain_row, w_b, part_head, tabs_head, b_gain, b_kinds,
                             dil=1, name="proj_b").reshape(bsz * s, B_COLS)
    idx = _norm_proj_heads(x2d, gain_row, _pad_cols([w_qi, w_ki], IDX_COLS, MXU_DTYPE), part_idx,
                           tabs_idx, jnp.ones((1, IDX_COLS), F32), [ROPE] * (IDX_COLS // EP_CHUNK),
                           dil=1, name="proj_idx").reshape(bsz * s, IDX_COLS)
    gate_col0 = 2 * d
    wi_col0 = gate_col0 + N_BRANCH * d
    feat_cols = -(-(wi_col0 + IDX_HEADS) // 512) * 512
    feat = _norm_proj(x2d, gain_row, _pad_cols([w_xc, w_yc, w_g, w_wi], feat_cols, MXU_DTYPE),
                      tn=512, out_dtype=F32, name="proj_feat")
    wi = feat[:, wi_col0:wi_col0 + IDX_HEADS].reshape(bsz, s // QB_BLOCK, QB_BLOCK, IDX_HEADS)
    out_b = _sparse_mixer(bproj, idx, wi.transpose(0, 1, 3, 2), bsz, s)

    out_c = _lru_mixer(feat, bsz, s, d, lru_conv_w, lru_conv_b, lru_wa, lru_ba, lru_wx, lru_bx,
                       lru_lambda)

    merged = _gate_merge(out_a, out_b, out_c, w_proj_a.astype(MXU_DTYPE),
                         w_proj_b.astype(MXU_DTYPE), w_proj_c.astype(MXU_DTYPE), feat, gate_col0,
                         b_gate.reshape(1, N_BRANCH * d).astype(F32), d)
    return _matmul_residual(merged, w_out.astype(MXU_DTYPE), x2d, "out_proj_residual")


def _conv_ffn_block(x2d, s, norm_ffn, ffn_w_up, ffn_conv_w, ffn_conv_b, ffn_w_down):
    d = x2d.shape[1]
    act = _ffn_up(x2d, s, norm_ffn.reshape(1, d).astype(F32), ffn_w_up.astype(MXU_DTYPE),
                  ffn_conv_w.astype(F32), ffn_conv_b.reshape(1, -1).astype(F32))
    return _matmul_residual(act, ffn_w_down.astype(MXU_DTYPE), x2d, "ffn_down_residual")


def kernel(x, positions, w_in, b_gate, qn_a, kn_a, qn_b, kn_b, lru_conv_w, lru_conv_b, lru_wa,
           lru_ba, lru_wx, lru_bx, lru_lambda, w_proj_a, w_proj_b, w_proj_c, w_out, norm_mix,
           norm_ffn, ffn_w_up, ffn_conv_w, ffn_conv_b, ffn_w_down):
    bsz, s, d = x.shape
    assert s % (DIL_GROUPS[-1][1] * A_BLOCK) == 0 and s % (2 * KEY_TILE) == 0
    x2d = x.reshape(bsz * s, d)
    pos_col = positions.astype(F32).reshape(bsz * s, 1)
    tabs_head = _rope_tables(pos_col, ROT_DIM, HEAD_DIM)
    tabs_idx = _rope_tables(pos_col, IDX_ROT, IDX_DIM)
    for layer in range(w_in.shape[0]):
        x2d = _temporal_block(
            x2d, bsz, s, tabs_head, tabs_idx, w_in[layer], b_gate[layer], qn_a[layer],
            kn_a[layer], qn_b[layer], kn_b[layer], lru_conv_w[layer], lru_conv_b[layer],
            lru_wa[layer], lru_ba[layer], lru_wx[layer], lru_bx[layer], lru_lambda[layer],
            w_proj_a[layer], w_proj_b[layer], w_proj_c[layer], w_out[layer], norm_mix[layer])
        x2d = _conv_ffn_block(x2d, s, norm_ffn[layer], ffn_w_up[layer], ffn_conv_w[layer],
                              ffn_conv_b[layer], ffn_w_down[layer])
    return x2d.reshape(bsz, s, d)
```

```python
import functools
import math

import numpy as np
import jax
import jax.numpy as jnp
from jax import lax
from jax.experimental import pallas as pl
from jax.experimental.pallas import tpu as pltpu

F32 = jnp.float32
MXU_DTYPE = jnp.bfloat16
SOFTMAX_DTYPE = jnp.float32

LANES = 128
SUBLANES = 8
VMEM_LIMIT_BYTES = 58 * 1024 * 1024

HEAD_DIM = 128
ROPE_THETA = 500000.0
ROT_DIM = HEAD_DIM // 4
NORM_EPS = 1e-6
DIL_GROUPS = ((128, 1), (512, 4), (2048, 16))
A_HEADS_PER_GROUP = 4
A_HEADS = A_HEADS_PER_GROUP * len(DIL_GROUPS)
A_BLOCK = 128
B_HEADS = 12
B_TOPK_MAX = 256
IDX_HEADS = 16
IDX_DIM = 64
IDX_ROT = IDX_DIM // 4
LRU_BLOCKS = 16
LRU_C = 8.0
LRU_CONV_W = 4
FFN_CONV_W = 3
N_BRANCH = 3
A_GROUP_WIDTH = A_HEADS_PER_GROUP * HEAD_DIM

B_COLS = 2048
KB_COL = 1536
VB_COL = 1792
IDX_COLS = 1280
EP_CHUNK = 256
PROJ_GROUP = 512
QB_BLOCK = 128
KEY_TILE = 256
STEPS_PER_BODY = 4
INT_MIN = -(2 ** 31)
KEY_NEG_INF = INT_MIN + 0x7FFFFF
HI16_MASK = -65536
BF16_ROWS = 16
COUNT_CHUNK = 1024
BF16_MIN_NORMAL = 0x80
F32_MIN_NORMAL = 1.1754943508222875e-38
NEG_BIG = -(2.0 ** 100)
LOG2_E = 1.4426950408889634


def _params(*semantics):
    return pltpu.CompilerParams(dimension_semantics=semantics,
                                vmem_limit_bytes=VMEM_LIMIT_BYTES)


def _gelu_tanh(x):
    return 0.5 * x * (1.0 + jnp.tanh(0.7978845608028654 * (x + 0.044715 * (x * x * x))))


def _sigmoid(x):
    return 1.0 / (1.0 + jnp.exp(-x))


def _rope_tables_kernel(pos_ref, par_ref, c_ref, s_ref):
    ang = pos_ref[...] * par_ref[0:1, :]
    c_ref[...] = jnp.where(par_ref[1:2, :] > 0.0, jnp.cos(ang), 1.0)
    s_ref[...] = jnp.sin(ang) * par_ref[2:3, :]


def _rope_tables(pos_col, rot_dim, period):
    t = pos_col.shape[0]
    half = rot_dim // 2
    freq = ROPE_THETA ** (-jnp.arange(half, dtype=F32) / half)
    lane = np.arange(LANES) % period
    in_rot = lane < rot_dim
    freq_lane = jnp.where(jnp.asarray(in_rot), freq[lane % half], 0.0)
    sign = np.where(lane < half, -1.0, np.where(in_rot, 1.0, 0.0))
    rows = [freq_lane, jnp.asarray(in_rot, F32), jnp.asarray(sign, F32)]
    par = jnp.stack(rows + [jnp.zeros((LANES,), F32)] * 5, axis=0)
    tm = min(t, 2048)
    spec = pl.BlockSpec((tm, LANES), lambda i: (i, 0))
    return pl.pallas_call(
        _rope_tables_kernel,
        grid=(t // tm,),
        in_specs=[pl.BlockSpec((tm, 1), lambda i: (i, 0)),
                  pl.BlockSpec((SUBLANES, LANES), lambda i: (0, 0))],
        out_specs=[spec, spec],
        out_shape=[jax.ShapeDtypeStruct((t, LANES), F32)] * 2,
        compiler_params=_params("parallel"),
        name="rope_tables",
    )(pos_col, par)


def _partner_matrix(head, half):
    m = np.zeros((EP_CHUNK, EP_CHUNK), np.float32)
    for base in range(0, EP_CHUNK, head):
        for i in range(half):
            m[base + i + half, base + i] = 1.0
            m[base + i, base + i + half] = 1.0
    return jnp.asarray(m, MXU_DTYPE)


def _normed(x_ref, g_ref):
    xf = x_ref[...]
    ms = jnp.mean(xf * xf, axis=-1, keepdims=True)
    return (xf * lax.rsqrt(ms + NORM_EPS) * g_ref[...]).astype(MXU_DTYPE)


def _norm_proj_kernel(x_ref, g_ref, w_ref, o_ref, h_ref):
    @pl.when(pl.program_id(1) == 0)
    def _():
        h_ref[...] = _normed(x_ref, g_ref)

    o_ref[...] = jnp.dot(h_ref[...], w_ref[...], preferred_element_type=F32).astype(o_ref.dtype)


def _norm_proj(x2d, gain, w, *, tn, out_dtype, name):
    t, d = x2d.shape
    n = w.shape[1]
    tm = min(t, 1024)
    return pl.pallas_call(
        _norm_proj_kernel,
        grid=(t // tm, n // tn),
        in_specs=[pl.BlockSpec((tm, d), lambda i, j: (i, 0)),
                  pl.BlockSpec((1, d), lambda i, j: (0, 0)),
                  pl.BlockSpec((d, tn), lambda i, j: (0, j))],
        out_specs=pl.BlockSpec((tm, tn), lambda i, j: (i, j)),
        out_shape=jax.ShapeDtypeStruct((t, n), out_dtype),
        scratch_shapes=[pltpu.VMEM((tm, d), MXU_DTYPE)],
        compiler_params=_params("parallel", "arbitrary"),
        name=name,
    )(x2d, gain, w)


NORM_ROPE, ROPE, PLAIN = "norm_rope", "rope", "plain"


def _norm_proj_heads_kernel(dil, kinds, x_ref, g_ref, w_ref, part_ref, c_ref, s_ref, hg_ref,
                            o_ref, y_buf):
    h = _normed(x_ref, g_ref)
    per_chunk = EP_CHUNK // LANES
    cos2 = jnp.concatenate([c_ref[...]] * per_chunk, axis=1)
    sin2 = jnp.concatenate([s_ref[...]] * per_chunk, axis=1)
    n = w_ref.shape[1]
    for g0 in range(0, n, PROJ_GROUP):
        width = min(PROJ_GROUP, n - g0)
        yg = jnp.dot(h, w_ref[:, g0:g0 + width], preferred_element_type=F32)
        for c0 in range(0, width, EP_CHUNK):
            kind = kinds[(g0 + c0) // EP_CHUNK]
            y = yg[:, c0:c0 + EP_CHUNK]
            if kind == NORM_ROPE:
                heads = []
                for k in range(EP_CHUNK // HEAD_DIM):
                    yh = y[:, k * HEAD_DIM:(k + 1) * HEAD_DIM]
                    ms = jnp.mean(yh * yh, axis=-1, keepdims=True)
                    lo = g0 + c0 + k * HEAD_DIM
                    heads.append(yh * lax.rsqrt(ms + NORM_EPS) * hg_ref[:, lo:lo + HEAD_DIM])
                y = jnp.concatenate(heads, axis=1)
            if kind != PLAIN:
                partner = jnp.dot(y.astype(MXU_DTYPE), part_ref[...], preferred_element_type=F32)
                y = y * cos2 + partner * sin2
            for k in range(per_chunk):
                y_buf[(g0 + c0) // LANES + k] = y[:, k * LANES:(k + 1) * LANES]

    rows = y_buf.shape[1] // dil
    for r in range(dil):
        for k in range(y_buf.shape[0]):
            slab = y_buf[k] if dil == 1 else y_buf[k, pl.ds(r, rows, stride=dil), :]
            o_ref[r, :, k * LANES:(k + 1) * LANES] = slab.astype(o_ref.dtype)


def _norm_proj_heads(x2d, gain, w, partner, tabs, head_gain, kinds, *, dil, name):
    t, d = x2d.shape
    n = w.shape[1]
    tm = min(t, 512)
    assert len(kinds) == n // EP_CHUNK
    tab = pl.BlockSpec((tm, LANES), lambda i: (i, 0))
    return pl.pallas_call(
        functools.partial(_norm_proj_heads_kernel, dil, tuple(kinds)),
        grid=(t // tm,),
        in_specs=[pl.BlockSpec((tm, d), lambda i: (i, 0)),
                  pl.BlockSpec((1, d), lambda i: (0, 0)),
                  pl.BlockSpec((d, n), lambda i: (0, 0)),
                  pl.BlockSpec((EP_CHUNK, EP_CHUNK), lambda i: (0, 0)),
                  tab, tab,
                  pl.BlockSpec((1, n), lambda i: (0, 0))],
        out_specs=pl.BlockSpec((dil, tm // dil, n), lambda i: (0, i, 0)),
        out_shape=jax.ShapeDtypeStruct((dil, t // dil, n), MXU_DTYPE),
        scratch_shapes=[pltpu.VMEM((n // LANES, tm, LANES), F32)],
        compiler_params=_params("parallel"),
        name=name,
    )(x2d, gain, w, partner, tabs[0], tabs[1], head_gain)


def _dilated_kernel(span, dil, q_ref, kc_ref, kp_ref, vc_ref, vp_ref, o_ref, lse_ref):
    i = pl.program_id(1)
    qi = lax.broadcasted_iota(jnp.int32, (A_BLOCK, 2 * A_BLOCK), 0) + A_BLOCK
    kj = lax.broadcasted_iota(jnp.int32, (A_BLOCK, 2 * A_BLOCK), 1)
    dist = qi - kj
    min_key = jnp.where(i > 0, 0, A_BLOCK)
    bias = jnp.where((dist >= 0) & (dist <= span) & (kj >= min_key), 0.0, NEG_BIG)

    def residue(r, carry):
        rows = pl.ds(r, A_BLOCK, stride=dil) if dil > 1 else slice(None)
        for h in range(A_HEADS_PER_GROUP):
            hs = slice(h * HEAD_DIM, (h + 1) * HEAD_DIM)
            q = q_ref[r, :, hs]
            k2 = jnp.concatenate([kp_ref[r, :, hs], kc_ref[r, :, hs]], axis=0)
            v2 = jnp.concatenate([vp_ref[r, :, hs], vc_ref[r, :, hs]], axis=0)
            s = lax.dot_general(q, k2, (((1,), (1,)), ((), ())), preferred_element_type=F32) + bias
            m = jnp.max(s, axis=-1, keepdims=True)
            p = jnp.exp(s - m)
            l = jnp.sum(p, axis=-1, keepdims=True)
            o = jnp.dot(p.astype(v2.dtype), v2, preferred_element_type=F32)
            o_ref[h, rows, :] = o / l
            lse_ref[h, rows, :] = jnp.broadcast_to(m + jnp.log(l), (A_BLOCK, HEAD_DIM))
        return carry

    lax.fori_loop(0, dil, residue, 0)


def _dilated_group(qkv, bsz, s, window, dil):
    n = s // dil
    nb = n // A_BLOCK
    blk = (dil, A_BLOCK, A_GROUP_WIDTH)

    def prev(i):
        return jnp.maximum(i - 1, 0)

    in_specs = [
        pl.BlockSpec(blk, lambda b, i: (0, b * nb + i, 0)),
        pl.BlockSpec(blk, lambda b, i: (0, b * nb + i, 1)),
        pl.BlockSpec(blk, lambda b, i: (0, b * nb + prev(i), 1)),
        pl.BlockSpec(blk, lambda b, i: (0, b * nb + i, 2)),
        pl.BlockSpec(blk, lambda b, i: (0, b * nb + prev(i), 2)),
    ]
    out_spec = pl.BlockSpec((A_HEADS_PER_GROUP, dil * A_BLOCK, HEAD_DIM), lambda b, i: (0, b * nb + i, 0))
    out_sds = jax.ShapeDtypeStruct((A_HEADS_PER_GROUP, bsz * s, HEAD_DIM), F32)
    return pl.pallas_call(
        functools.partial(_dilated_kernel, window // dil, dil),
        grid=(bsz, nb),
        in_specs=in_specs,
        out_specs=[out_spec, out_spec],
        out_shape=[out_sds, out_sds],
        compiler_params=_params("parallel", "arbitrary"),
        name=f"dilated_attn_d{dil}",
    )(qkv, qkv, qkv, qkv, qkv)


def _merge_groups_kernel(o0, o1, o2, l0, l1, l2, out_ref):
    for h in range(A_HEADS_PER_GROUP):
        a, b, c = l0[h], l1[h], l2[h]
        m = jnp.maximum(jnp.maximum(a, b), c)
        ea, eb, ec = jnp.exp(a - m), jnp.exp(b - m), jnp.exp(c - m)
        num = ea * o0[h] + eb * o1[h] + ec * o2[h]
        out_ref[:, h * HEAD_DIM:(h + 1) * HEAD_DIM] = (num / (ea + eb + ec)).astype(out_ref.dtype)


def _merge_groups(os_, lses):
    t = os_[0].shape[1]
    tm = min(t, 1024)
    spec = pl.BlockSpec((A_HEADS_PER_GROUP, tm, HEAD_DIM), lambda i: (0, i, 0))
    return pl.pallas_call(
        _merge_groups_kernel,
        grid=(t // tm,),
        in_specs=[spec] * 6,
        out_specs=pl.BlockSpec((tm, A_GROUP_WIDTH), lambda i: (i, 0)),
        out_shape=jax.ShapeDtypeStruct((t, A_GROUP_WIDTH), MXU_DTYPE),
        compiler_params=_params("parallel"),
        name="dilated_merge",
    )(*os_, *lses)


def _tree_row_sum(x, group):
    parts = [x[g * group:(g + 1) * group, :] for g in range(x.shape[0] // group)]
    while len(parts) > 1:
        parts = [a + b for a, b in zip(parts[0::2], parts[1::2])]
    return parts[0]


def _sparse_kernel(topk, idx_scale, qit_ref, wt_ref, ki_ref, qbt_ref, kb_ref, vt_ref,
                   out_ref, keys_ref, hi_ref, acc_ref, m_ref, s_idx0, s_idx1, s_att0, s_att1,
                   p0, p1, al0, al1, bias_buf):
    qb = pl.program_id(1)
    t0 = qb * QB_BLOCK
    n_pairs = (t0 + QB_BLOCK + 2 * KEY_TILE - 1) // (2 * KEY_TILE)
    n_tiles = 2 * n_pairs
    last_tile = n_tiles - 1
    lane_q = lax.broadcasted_iota(jnp.int32, (KEY_TILE, QB_BLOCK), 1) + t0
    row_id = lax.broadcasted_iota(jnp.int32, (KEY_TILE, QB_BLOCK), 0)
    query_pos = lax.broadcasted_iota(jnp.int32, (1, QB_BLOCK), 1) + t0
    few_keys = query_pos < topk
    wts = wt_ref[...] * idx_scale

    def tile_start(kt):
        return pl.multiple_of(kt * KEY_TILE, KEY_TILE)

    def run_pipelined(steps):
        def body(j, carry):
            steps(j * STEPS_PER_BODY, STEPS_PER_BODY)
            return carry

        n_bodies = n_tiles // STEPS_PER_BODY
        lax.fori_loop(0, n_bodies, body, 0)
        if STEPS_PER_BODY > 2:
            @pl.when(n_tiles % STEPS_PER_BODY != 0)
            def _():
                steps(n_bodies * STEPS_PER_BODY, 2)

    def idx_matmul(kt, buf):
        k_tile = ki_ref[pl.ds(tile_start(kt), KEY_TILE), 0:IDX_DIM]
        buf[...] = jnp.dot(k_tile, qit_ref[...], preferred_element_type=F32)

    def idx_reduce(kt, buf):
        r0 = tile_start(kt)
        acc = jnp.zeros((KEY_TILE, QB_BLOCK), F32)
        for h in range(IDX_HEADS):
            acc = acc + jnp.maximum(buf[:, h * QB_BLOCK:(h + 1) * QB_BLOCK], 0.0) * wts[h:h + 1, :]
        acc = jnp.where(jnp.abs(acc) < F32_MIN_NORMAL, 0.0, acc)
        acc = jnp.where(row_id + r0 <= lane_q, acc, -jnp.inf)
        bits = pltpu.bitcast(acc, jnp.int32)
        keys_ref[pl.ds(r0, KEY_TILE), :] = bits ^ ((bits >> 31) & 0x7FFFFFFF)
        hi_ref[pl.ds(r0, KEY_TILE), :] = pltpu.bitcast(bits & HI16_MASK, F32).astype(jnp.bfloat16)

    def score_steps(first, count):
        bufs = (s_idx0, s_idx1)
        for k in range(count):
            idx_matmul(jnp.minimum(first + k + 1, last_tile), bufs[(k + 1) % 2])
            idx_reduce(first + k, bufs[k % 2])

    idx_matmul(0, s_idx0)
    run_pipelined(score_steps)

    def sum_over_keys(partial, acc_rows, dtype):
        per_chunk = COUNT_CHUNK // KEY_TILE
        n_chunks = n_tiles // per_chunk

        def chunk(c, acc):
            return acc + partial(pl.multiple_of(c * COUNT_CHUNK, COUNT_CHUNK), COUNT_CHUNK)

        def tile(kt, acc):
            return acc + partial(tile_start(kt), KEY_TILE)

        acc = lax.fori_loop(0, n_chunks, chunk, jnp.zeros((acc_rows, QB_BLOCK), dtype))
        acc = lax.fori_loop(n_chunks * per_chunk, n_tiles, tile, acc)
        return jnp.sum(acc, axis=0, keepdims=True)

    def count_prefix_ge(prefix):
        prefix = jnp.where((prefix > 0) & (prefix < BF16_MIN_NORMAL), BF16_MIN_NORMAL, prefix)
        fbits = (prefix ^ ((prefix >> 15) & 0x7FFF)) << 16
        cand = pltpu.bitcast(fbits, F32).astype(jnp.bfloat16)

        def partial(r0, rows):
            hit = jnp.where(hi_ref[pl.ds(r0, rows), :] >= cand,
                            jnp.ones((rows, QB_BLOCK), jnp.bfloat16),
                            jnp.zeros((rows, QB_BLOCK), jnp.bfloat16))
            return _tree_row_sum(hit, BF16_ROWS).astype(F32)

        return sum_over_keys(partial, BF16_ROWS, F32)

    want = jnp.float32(topk)
    zero_prefix = jnp.zeros((1, QB_BLOCK), jnp.int32)
    cnt0 = count_prefix_ge(zero_prefix)
    take0 = cnt0 >= want
    prefix = jnp.where(take0, zero_prefix, -(2 ** 15))
    cnt = jnp.where(take0, cnt0, jnp.float32(2 ** 24))

    def prefix_step(i, carry):
        prefix, cnt = carry
        cand = prefix | lax.shift_left(jnp.int32(1), 14 - i)
        c = count_prefix_ge(cand)
        take = c >= want
        return jnp.where(take, cand, prefix), jnp.where(take, c, cnt)

    prefix, cnt = lax.fori_loop(0, 15, prefix_step, (prefix, cnt))

    def count_where(pred):
        def partial(r0, rows):
            rid = lax.broadcasted_iota(jnp.int32, (rows, QB_BLOCK), 0) + r0
            hit = jnp.where(pred(keys_ref[pl.ds(r0, rows), :], rid), 1, 0)
            return _tree_row_sum(hit, SUBLANES)
        return sum_over_keys(partial, SUBLANES, jnp.int32)

    def count_ge(cand):
        return count_where(lambda k, rid: k >= cand)

    def unresolved(cnt):
        return jnp.max(jnp.where(few_keys | (cnt == topk), 0, 1)) > 0

    def low_cond(carry):
        bit, _, cnt = carry
        return (bit >= 0) & unresolved(cnt)

    def low_step(carry):
        bit, tau, cnt = carry
        cand = tau | lax.shift_left(jnp.int32(1), bit)
        c = count_ge(cand)
        take = c >= topk
        return bit - 1, jnp.where(take, cand, tau), jnp.where(take, c, cnt)

    _, tau, n_ge = lax.while_loop(low_cond, low_step,
                                  (jnp.int32(15), prefix << 16, cnt.astype(jnp.int32)))
    tau = jnp.where(few_keys, KEY_NEG_INF + 1, tau)

    @pl.when(jnp.max(jnp.where(few_keys, 0, n_ge)) > topk)
    def _():
        need = topk - count_ge(tau + 1)

        def idx_step(i, last):
            cand = last | lax.shift_left(jnp.int32(1), 30 - i)
            below = count_where(lambda k, rid: (k == tau) & (rid < cand))
            return jnp.where(below < need, cand, last)

        last = lax.fori_loop(0, 31, idx_step, jnp.zeros((1, QB_BLOCK), jnp.int32))

        def demote(kt, carry):
            r0 = tile_start(kt)
            k = keys_ref[pl.ds(r0, KEY_TILE), :]
            drop = (k == tau) & (row_id + r0 > last)
            keys_ref[pl.ds(r0, KEY_TILE), :] = jnp.where(drop, tau - 1, k)
            return carry

        lax.fori_loop(0, n_tiles, demote, 0)

    def att_matmul(kt, s_buf):
        s = jnp.dot(kb_ref[pl.ds(tile_start(kt), KEY_TILE), :], qbt_ref[...],
                    preferred_element_type=F32)
        s_buf[...] = s.astype(s_buf.dtype)

    def softmax_tile(kt, s_buf, p_buf, al_buf):
        sel = keys_ref[pl.ds(tile_start(kt), KEY_TILE), :] >= tau
        bias_buf[...] = jnp.where(sel, 0.0, NEG_BIG).astype(bias_buf.dtype)
        for h in range(B_HEADS):
            hs = slice(h * QB_BLOCK, (h + 1) * QB_BLOCK)
            s = s_buf[:, hs] + bias_buf[...]
            m_old = m_ref[:, hs]
            m_new = jnp.maximum(m_old, jnp.max(s, axis=0, keepdims=True).astype(F32))
            m_ref[:, hs] = m_new
            al_buf[:, hs] = jnp.exp2(m_old - m_new)
            p_buf[:, hs] = jnp.exp2(s - m_new.astype(s.dtype)).astype(p_buf.dtype)

    def value_matmul(kt, p_buf, al_buf):
        pv = jnp.dot(vt_ref[:, pl.ds(tile_start(kt), KEY_TILE)], p_buf[...],
                     preferred_element_type=F32)
        acc_ref[...] = acc_ref[...] * al_buf[...] + pv

    m_ref[...] = jnp.full(m_ref.shape, NEG_BIG, F32)
    acc_ref[...] = jnp.zeros(acc_ref.shape, F32)
    p1[...] = jnp.zeros(p1.shape, p1.dtype)
    al1[...] = jnp.ones(al1.shape, F32)
    att_matmul(0, s_att0)

    def attend_steps(first, count):
        s_bufs, p_bufs, al_bufs = (s_att0, s_att1), (p0, p1), (al0, al1)
        for k in range(count):
            cur, oth = k % 2, (k + 1) % 2
            att_matmul(jnp.minimum(first + k + 1, last_tile), s_bufs[oth])
            value_matmul(jnp.maximum(first + k - 1, 0), p_bufs[oth], al_bufs[oth])
            softmax_tile(first + k, s_bufs[cur], p_bufs[cur], al_bufs[cur])

    run_pipelined(attend_steps)
    value_matmul(last_tile, p1, al1)

    for h in range(B_HEADS):
        hs = slice(h * QB_BLOCK, (h + 1) * QB_BLOCK)
        o_t = acc_ref[0:HEAD_DIM, hs] / acc_ref[HEAD_DIM:HEAD_DIM + 1, hs]
        out_ref[:, h * HEAD_DIM:(h + 1) * HEAD_DIM] = o_t.T.astype(out_ref.dtype)


def _sparse_mixer(bproj, idx, wi_t, bsz, s):
    nqb = s // QB_BLOCK
    topk = min(B_TOPK_MAX, s // 4)
    idx_scale = (IDX_HEADS ** -0.5) * (IDX_DIM ** -0.5)
    nq = B_HEADS * QB_BLOCK
    qi = idx[:, :IDX_HEADS * IDX_DIM].reshape(bsz, nqb, QB_BLOCK, IDX_HEADS, IDX_DIM)
    qit = qi.transpose(0, 1, 4, 3, 2).reshape(bsz, nqb, IDX_DIM, IDX_HEADS * QB_BLOCK)
    qbq = bproj[:, :B_HEADS * HEAD_DIM].reshape(bsz, nqb, QB_BLOCK, B_HEADS, HEAD_DIM)
    qbt = qbq.transpose(0, 1, 4, 3, 2).reshape(bsz, nqb, HEAD_DIM, nq)
    vb = bproj[:, VB_COL:VB_COL + HEAD_DIM].reshape(bsz, s, HEAD_DIM)
    vt = jnp.concatenate([vb.transpose(0, 2, 1),
                          jnp.ones((bsz, 1, s), vb.dtype),
                          jnp.zeros((bsz, SUBLANES - 1, s), vb.dtype)], axis=1)
    v_rows = HEAD_DIM + SUBLANES
    b3 = bproj.reshape(bsz, s, B_COLS)
    idx3 = idx.reshape(bsz, s, IDX_COLS)
    kb_blk = KB_COL // LANES
    ki_blk = (IDX_HEADS * IDX_DIM) // LANES
    out = pl.pallas_call(
        functools.partial(_sparse_kernel, topk, idx_scale),
        grid=(bsz, nqb),
        in_specs=[
            pl.BlockSpec((None, None, IDX_DIM, IDX_HEADS * QB_BLOCK), lambda b, q: (b, q, 0, 0)),
            pl.BlockSpec((None, None, IDX_HEADS, QB_BLOCK), lambda b, q: (b, q, 0, 0)),
            pl.BlockSpec((None, s, LANES), lambda b, q: (b, 0, ki_blk)),
            pl.BlockSpec((None, None, HEAD_DIM, nq), lambda b, q: (b, q, 0, 0)),
            pl.BlockSpec((None, s, LANES), lambda b, q: (b, 0, kb_blk)),
            pl.BlockSpec((None, v_rows, s), lambda b, q: (b, 0, 0)),
        ],
        out_specs=pl.BlockSpec((None, QB_BLOCK, B_HEADS * HEAD_DIM), lambda b, q: (b, q, 0)),
        out_shape=jax.ShapeDtypeStruct((bsz, s, B_HEADS * HEAD_DIM), MXU_DTYPE),
        scratch_shapes=[pltpu.VMEM((s, QB_BLOCK), jnp.int32),
                        pltpu.VMEM((s, QB_BLOCK), jnp.bfloat16),
                        pltpu.VMEM((v_rows, nq), F32),
                        pltpu.VMEM((1, nq), F32),
                        pltpu.VMEM((KEY_TILE, IDX_HEADS * QB_BLOCK), F32),
                        pltpu.VMEM((KEY_TILE, IDX_HEADS * QB_BLOCK), F32),
                        pltpu.VMEM((KEY_TILE, nq), SOFTMAX_DTYPE),
                        pltpu.VMEM((KEY_TILE, nq), SOFTMAX_DTYPE),
                        pltpu.VMEM((KEY_TILE, nq), MXU_DTYPE),
                        pltpu.VMEM((KEY_TILE, nq), MXU_DTYPE),
                        pltpu.VMEM((1, nq), F32),
                        pltpu.VMEM((1, nq), F32),
                        pltpu.VMEM((KEY_TILE, QB_BLOCK), SOFTMAX_DTYPE)],
        compiler_params=_params("parallel", "arbitrary"),
        name="sparse_mixer",
    )(qit, wi_t, idx3, qbt, b3, vt)
    return out.reshape(bsz * s, B_HEADS * HEAD_DIM)


def _lru_kernel(ts, xc_ref, yc_ref, cw_ref, cb_ref, wa_ref, wx_ref, ba_ref, bx_ref, lam_ref,
                o_ref, xbuf, a_buf, b_buf, h_state):
    halo = SUBLANES

    @pl.when(pl.program_id(1) == 0)
    def _():
        xbuf[0:halo, :] = jnp.zeros((halo, xbuf.shape[1]), F32)
        h_state[...] = jnp.zeros(h_state.shape, F32)

    xbuf[halo:halo + ts, :] = xc_ref[...]
    xc = cb_ref[...] + cw_ref[0:1, :] * xbuf[halo - 3:halo - 3 + ts, :]
    for tap in range(1, LRU_CONV_W):
        lo = halo - (LRU_CONV_W - 1) + tap
        xc = xc + cw_ref[tap:tap + 1, :] * xbuf[lo:lo + ts, :]
    xbuf[0:halo, :] = xc_ref[ts - halo:ts, :]

    z = -lam_ref[...]
    softplus = jnp.maximum(z, 0.0) + jnp.log1p(jnp.exp(-jnp.abs(z)))
    bw = xc.shape[1] // LRU_BLOCKS
    for blk in range(LRU_BLOCKS):
        cs = slice(blk * bw, (blk + 1) * bw)
        xs = xc[:, cs]
        xs_m = xs.astype(MXU_DTYPE)
        r = _sigmoid(jnp.dot(xs_m, wa_ref[blk], preferred_element_type=F32) + ba_ref[:, cs])
        gate_i = _sigmoid(jnp.dot(xs_m, wx_ref[blk], preferred_element_type=F32) + bx_ref[:, cs])
        log_a = (-LRU_C) * r * softplus[:, cs]
        a = jnp.exp(log_a)
        a_buf[:, cs] = a
        b_buf[:, cs] = jnp.sqrt(-jnp.tanh(log_a) * (a * a + 1.0)) * gate_i * xs

    def step(t, h):
        h = a_buf[pl.ds(t, 1), :] * h + b_buf[pl.ds(t, 1), :]
        b_buf[pl.ds(t, 1), :] = h
        return h

    h_state[0:1, :] = lax.fori_loop(0, ts, step, h_state[0:1, :], unroll=8)
    o_ref[...] = (b_buf[...] * _gelu_tanh(yc_ref[...])).astype(o_ref.dtype)


def _lru_mixer(feat, bsz, s, width, conv_w, conv_b, wa, ba, wx, bx, lam):
    ts = min(s, 512)
    nt = s // ts
    row = lambda a: a.reshape(1, width).astype(F32)
    vec = pl.BlockSpec((1, width), lambda b, t: (0, 0))
    wspec = pl.BlockSpec(wa.shape, lambda b, t: (0, 0, 0))
    return pl.pallas_call(
        functools.partial(_lru_kernel, ts),
        grid=(bsz, nt),
        in_specs=[pl.BlockSpec((ts, width), lambda b, t: (b * nt + t, 0)),
                  pl.BlockSpec((ts, width), lambda b, t: (b * nt + t, 1)),
                  pl.BlockSpec((LRU_CONV_W, width), lambda b, t: (0, 0)),
                  vec, wspec, wspec, vec, vec, vec],
        out_specs=pl.BlockSpec((ts, width), lambda b, t: (b * nt + t, 0)),
        out_shape=jax.ShapeDtypeStruct((bsz * s, width), MXU_DTYPE),
        scratch_shapes=[pltpu.VMEM((ts + SUBLANES, width), F32),
                        pltpu.VMEM((ts, width), F32),
                        pltpu.VMEM((ts, width), F32),
                        pltpu.VMEM((SUBLANES, width), F32)],
        compiler_params=_params("parallel", "arbitrary"),
        name="rglru_mixer",
    )(feat, feat, conv_w.astype(F32), row(conv_b), wa.astype(MXU_DTYPE), wx.astype(MXU_DTYPE),
      row(ba), row(bx), row(lam))


def _gate_merge_kernel(oa, ob, oc, wa, wb, wc, g0, g1, g2, b0, b1, b2, out_ref):
    acc = _sigmoid(g0[...] + b0[...]) * jnp.dot(oa[...], wa[...], preferred_element_type=F32)
    acc = acc + _sigmoid(g1[...] + b1[...]) * jnp.dot(ob[...], wb[...], preferred_element_type=F32)
    acc = acc + _sigmoid(g2[...] + b2[...]) * jnp.dot(oc[...], wc[...], preferred_element_type=F32)
    out_ref[...] = acc.astype(out_ref.dtype)


def _gate_merge(out_a, out_b, out_c, wpa, wpb, wpc, feat, gate_col0, b_gate_row, d_model):
    t = out_a.shape[0]
    tm = min(t, 1024)
    tn = 512
    nj = d_model // tn
    g_blk = gate_col0 // tn

    def act(a):
        return pl.BlockSpec((tm, a.shape[1]), lambda i, j: (i, 0))

    def wgt(w):
        return pl.BlockSpec((w.shape[0], tn), lambda i, j: (0, j))

    gates = [pl.BlockSpec((tm, tn), functools.partial(lambda br, i, j: (i, g_blk + br * nj + j), br))
             for br in range(N_BRANCH)]
    biases = [pl.BlockSpec((1, tn), functools.partial(lambda br, i, j: (0, br * nj + j), br))
              for br in range(N_BRANCH)]
    return pl.pallas_call(
        _gate_merge_kernel,
        grid=(t // tm, nj),
        in_specs=[act(out_a), act(out_b), act(out_c), wgt(wpa), wgt(wpb), wgt(wpc)] + gates + biases,
        out_specs=pl.BlockSpec((tm, tn), lambda i, j: (i, j)),
        out_shape=jax.ShapeDtypeStruct((t, d_model), MXU_DTYPE),
        compiler_params=_params("parallel", "arbitrary"),
        name="gate_merge",
    )(out_a, out_b, out_c, wpa, wpb, wpc, feat, feat, feat, b_gate_row, b_gate_row, b_gate_row)


def _matmul_residual_kernel(a_ref, w_ref, r_ref, o_ref):
    o_ref[...] = r_ref[...] + jnp.dot(a_ref[...], w_ref[...], preferred_element_type=F32)


def _matmul_residual(a, w, resid, name):
    t, k = a.shape
    n = w.shape[1]
    tm = min(t, 1024)
    tn = 512
    return pl.pallas_call(
        _matmul_residual_kernel,
        grid=(t // tm, n // tn),
        in_specs=[pl.BlockSpec((tm, k), lambda i, j: (i, 0)),
                  pl.BlockSpec((k, tn), lambda i, j: (0, j)),
                  pl.BlockSpec((tm, tn), lambda i, j: (i, j))],
        out_specs=pl.BlockSpec((tm, tn), lambda i, j: (i, j)),
        out_shape=jax.ShapeDtypeStruct((t, n), F32),
        compiler_params=_params("parallel", "arbitrary"),
        name=name,
    )(a, w, resid)


def _ffn_up_kernel(tm, tiles_per_seq, x_ref, xh_ref, g_ref, wa_ref, wb_ref, cwa_ref, cwb_ref,
                   cba_ref, cbb_ref, o_ref, h_ref, hh_ref, ua_buf, ub_buf):
    halo = SUBLANES

    def normed(ref):
        xf = ref[...]
        ms = jnp.mean(xf * xf, axis=-1, keepdims=True)
        return (xf * lax.rsqrt(ms + NORM_EPS) * g_ref[...]).astype(MXU_DTYPE)

    @pl.when(pl.program_id(1) == 0)
    def _():
        h_ref[...] = normed(x_ref)
        hh_ref[...] = normed(xh_ref)

    keep = jnp.where(pl.program_id(0) % tiles_per_seq == 0, 0.0, 1.0)

    def conv_half(w_ref, cw_ref, cb_ref, buf):
        buf[0:halo, :] = keep * jnp.dot(hh_ref[...], w_ref[...], preferred_element_type=F32)
        buf[halo:halo + tm, :] = jnp.dot(h_ref[...], w_ref[...], preferred_element_type=F32)
        y = cb_ref[...] + cw_ref[0:1, :] * buf[halo - 2:halo - 2 + tm, :]
        for tap in range(1, FFN_CONV_W):
            lo = halo - (FFN_CONV_W - 1) + tap
            y = y + cw_ref[tap:tap + 1, :] * buf[lo:lo + tm, :]
        return y

    ya = conv_half(wa_ref, cwa_ref, cba_ref, ua_buf)
    yb = conv_half(wb_ref, cwb_ref, cbb_ref, ub_buf)
    o_ref[...] = (_gelu_tanh(ya) * yb).astype(o_ref.dtype)


def _ffn_up(x2d, s, gain, w_up, conv_w, conv_b_row):
    t, d = x2d.shape
    d_ff = w_up.shape[1] // 2
    tm = min(s, 1024)
    tn = 512
    nj = d_ff // tn
    hb = tm // SUBLANES
    return pl.pallas_call(
        functools.partial(_ffn_up_kernel, tm, s // tm),
        grid=(t // tm, nj),
        in_specs=[pl.BlockSpec((tm, d), lambda i, j: (i, 0)),
                  pl.BlockSpec((SUBLANES, d), lambda i, j: (jnp.maximum(i * hb - 1, 0), 0)),
                  pl.BlockSpec((1, d), lambda i, j: (0, 0)),
                  pl.BlockSpec((d, tn), lambda i, j: (0, j)),
                  pl.BlockSpec((d, tn), lambda i, j: (0, nj + j)),
                  pl.BlockSpec((FFN_CONV_W, tn), lambda i, j: (0, j)),
                  pl.BlockSpec((FFN_CONV_W, tn), lambda i, j: (0, nj + j)),
                  pl.BlockSpec((1, tn), lambda i, j: (0, j)),
                  pl.BlockSpec((1, tn), lambda i, j: (0, nj + j))],
        out_specs=pl.BlockSpec((tm, tn), lambda i, j: (i, j)),
        out_shape=jax.ShapeDtypeStruct((t, d_ff), MXU_DTYPE),
        scratch_shapes=[pltpu.VMEM((tm, d), MXU_DTYPE),
                        pltpu.VMEM((SUBLANES, d), MXU_DTYPE),
                        pltpu.VMEM((tm + SUBLANES, tn), F32),
                        pltpu.VMEM((tm + SUBLANES, tn), F32)],
        compiler_params=_params("parallel", "arbitrary"),
        name="ffn_up_conv_gate",
    )(x2d, x2d, gain, w_up, w_up, conv_w, conv_w, conv_b_row, conv_b_row)


def _pad_cols(parts, total, dtype):
    width = sum(p.shape[1] for p in parts)
    rows = parts[0].shape[0]
    if total > width:
        parts = list(parts) + [jnp.zeros((rows, total - width), parts[0].dtype)]
    return jnp.concatenate(parts, axis=1).astype(dtype)


def _temporal_block(x2d, bsz, s, tabs_head, tabs_idx, w_in, b_gate, qn_a, kn_a, qn_b, kn_b,
                    lru_conv_w, lru_conv_b, lru_wa, lru_ba, lru_wx, lru_bx, lru_lambda,
                    w_proj_a, w_proj_b, w_proj_c, w_out, norm_mix):
    d = x2d.shape[1]
    a_w = A_HEADS * HEAD_DIM
    b_w = B_HEADS * HEAD_DIM
    sizes = (a_w, a_w, a_w, b_w, HEAD_DIM, HEAD_DIM, IDX_HEADS * IDX_DIM, IDX_DIM, IDX_HEADS,
             d, d, N_BRANCH * d)
    offs = np.concatenate([[0], np.cumsum(sizes)])
    col = lambda k: w_in[:, offs[k]:offs[k + 1]]
    w_qa, w_ka, w_va, w_qb, w_kb, w_vb, w_qi, w_ki, w_wi, w_xc, w_yc, w_g = [col(k) for k in range(12)]

    gain_row = norm_mix.reshape(1, d).astype(F32)
    att_scale = HEAD_DIM ** -0.5
    ones = lambda k: jnp.ones((k,), F32)
    part_head = _partner_matrix(HEAD_DIM, ROT_DIM // 2)
    part_idx = _partner_matrix(IDX_DIM, IDX_ROT // 2)
    gw = A_GROUP_WIDTH

    a_gain = jnp.concatenate([jnp.tile(qn_a.astype(F32) * att_scale, A_HEADS_PER_GROUP),
                              jnp.tile(kn_a.astype(F32), A_HEADS_PER_GROUP), ones(gw)]).reshape(1, 3 * gw)
    a_kinds = [NORM_ROPE] * 4 + [PLAIN] * 2
    os_, lses = [], []
    for g, (window, dil) in enumerate(DIL_GROUPS):
        gs = slice(g * gw, (g + 1) * gw)
        w_g3 = jnp.concatenate([w_qa[:, gs], w_ka[:, gs], w_va[:, gs]], axis=1).astype(MXU_DTYPE)
        qkv = _norm_proj_heads(x2d, gain_row, w_g3, part_head, tabs_head, a_gain, a_kinds,
                               dil=dil, name=f"proj_a_d{dil}")
        o, lse = _dilated_group(qkv, bsz, s, window, dil)
        os_.append(o)
        lses.append(lse)
    out_a = _merge_groups(os_, lses)

    b_gain = jnp.concatenate([jnp.tile(qn_b.astype(F32) * (att_scale * LOG2_E), B_HEADS),
                              kn_b.astype(F32), ones(B_COLS - b_w - HEAD_DIM)]).reshape(1, B_COLS)
    zeros = jnp.zeros((d, HEAD_DIM), w_in.dtype)
    w_b = jnp.concatenate([w_qb, w_kb, zeros, w_vb, zeros], axis=1).astype(MXU_DTYPE)
    b_kinds = [NORM_ROPE] * (b_w // EP_CHUNK + 1) + [PLAIN]
    bproj = _norm_proj_heads(x2d, gain_row, w_b, part_head, tabs_head, b_gain, b_kinds,
                             dil=1, name="proj_b").reshape(bsz * s, B_COLS)
    idx = _norm_proj_heads(x2d, gain_row, _pad_cols([w_qi, w_ki], IDX_COLS, MXU_DTYPE), part_idx,
                           tabs_idx, jnp.ones((1, IDX_COLS), F32), [ROPE] * (IDX_COLS // EP_CHUNK),
                           dil=1, name="proj_idx").reshape(bsz * s, IDX_COLS)
    gate_col0 = 2 * d
    wi_col0 = gate_col0 + N_BRANCH * d
    feat_cols = -(-(wi_col0 + IDX_HEADS) // 512) * 512
    feat = _norm_proj(x2d, gain_row, _pad_cols([w_xc, w_yc, w_g, w_wi], feat_cols, MXU_DTYPE),
                      tn=512, out_dtype=F32, name="proj_feat")
    wi = feat[:, wi_col0:wi_col0 + IDX_HEADS].reshape(bsz, s // QB_BLOCK, QB_BLOCK, IDX_HEADS)
    out_b = _sparse_mixer(bproj, idx, wi.transpose(0, 1, 3, 2), bsz, s)

    out_c = _lru_mixer(feat, bsz, s, d, lru_conv_w, lru_conv_b, lru_wa, lru_ba, lru_wx, lru_bx,
                       lru_lambda)

    merged = _gate_merge(out_a, out_b, out_c, w_proj_a.astype(MXU_DTYPE),
                         w_proj_b.astype(MXU_DTYPE), w_proj_c.astype(MXU_DTYPE), feat, gate_col0,
                         b_gate.reshape(1, N_BRANCH * d).astype(F32), d)
    return _matmul_residual(merged, w_out.astype(MXU_DTYPE), x2d, "out_proj_residual")


def _conv_ffn_block(x2d, s, norm_ffn, ffn_w_up, ffn_conv_w, ffn_conv_b, ffn_w_down):
    d = x2d.shape[1]
    act = _ffn_up(x2d, s, norm_ffn.reshape(1, d).astype(F32), ffn_w_up.astype(MXU_DTYPE),
                  ffn_conv_w.astype(F32), ffn_conv_b.reshape(1, -1).astype(F32))
    return _matmul_residual(act, ffn_w_down.astype(MXU_DTYPE), x2d, "ffn_down_residual")


def kernel(x, positions, w_in, b_gate, qn_a, kn_a, qn_b, kn_b, lru_conv_w, lru_conv_b, lru_wa,
           lru_ba, lru_wx, lru_bx, lru_lambda, w_proj_a, w_proj_b, w_proj_c, w_out, norm_mix,
           norm_ffn, ffn_w_up, ffn_conv_w, ffn_conv_b, ffn_w_down):
    bsz, s, d = x.shape
    assert s % (DIL_GROUPS[-1][1] * A_BLOCK) == 0 and s % (2 * KEY_TILE) == 0
    x2d = x.reshape(bsz * s, d)
    pos_col = positions.astype(F32).reshape(bsz * s, 1)
    tabs_head = _rope_tables(pos_col, ROT_DIM, HEAD_DIM)
    tabs_idx = _rope_tables(pos_col, IDX_ROT, IDX_DIM)
    for layer in range(w_in.shape[0]):
        x2d = _temporal_block(
            x2d, bsz, s, tabs_head, tabs_idx, w_in[layer], b_gate[layer], qn_a[layer],
            kn_a[layer], qn_b[layer], kn_b[layer], lru_conv_w[layer], lru_conv_b[layer],
            lru_wa[layer], lru_ba[layer], lru_wx[layer], lru_bx[layer], lru_lambda[layer],
            w_proj_a[layer], w_proj_b[layer], w_proj_c[layer], w_out[layer], norm_mix[layer])
        x2d = _conv_ffn_block(x2d, s, norm_ffn[layer], ffn_w_up[layer], ffn_conv_w[layer],
                              ffn_conv_b[layer], ffn_w_down[layer])
    return x2d.reshape(bsz, s, d)
```

```python
import functools
import math

import numpy as np
import jax
import jax.numpy as jnp
from jax import lax
from jax.experimental import pallas as pl
from jax.experimental.pallas import tpu as pltpu

F32 = jnp.float32
MXU_DTYPE = jnp.bfloat16
SOFTMAX_DTYPE = jnp.float32

LANES = 128
SUBLANES = 8
VMEM_LIMIT_BYTES = 58 * 1024 * 1024

HEAD_DIM = 128
ROPE_THETA = 500000.0
ROT_DIM = HEAD_DIM // 4
NORM_EPS = 1e-6
DIL_GROUPS = ((128, 1), (512, 4), (2048, 16))
A_HEADS_PER_GROUP = 4
A_HEADS = A_HEADS_PER_GROUP * len(DIL_GROUPS)
A_BLOCK = 128
B_HEADS = 12
B_TOPK_MAX = 256
IDX_HEADS = 16
IDX_DIM = 64
IDX_ROT = IDX_DIM // 4
LRU_BLOCKS = 16
LRU_C = 8.0
LRU_CONV_W = 4
FFN_CONV_W = 3
N_BRANCH = 3
A_GROUP_WIDTH = A_HEADS_PER_GROUP * HEAD_DIM

B_COLS = 2048
KB_COL = 1536
VB_COL = 1792
IDX_COLS = 1280
EP_CHUNK = 256
PROJ_GROUP = 512
FEAT_TILE = 1536
QB_BLOCK = 128
KEY_TILE = 256
STEPS_PER_BODY = 4
INT_MIN = -(2 ** 31)
KEY_NEG_INF = INT_MIN + 0x7FFFFF
HI16_MASK = -65536
BF16_ROWS = 16
COUNT_VREGS = 64
BF16_MIN_NORMAL = 0x80
F32_MIN_NORMAL = 1.1754943508222875e-38
NEG_BIG = -(2.0 ** 100)
LOG2_E = 1.4426950408889634


def _params(*semantics):
    return pltpu.CompilerParams(dimension_semantics=semantics,
                                vmem_limit_bytes=VMEM_LIMIT_BYTES)


def _gelu_tanh(x):
    return 0.5 * x * (1.0 + jnp.tanh(0.7978845608028654 * (x + 0.044715 * (x * x * x))))


def _sigmoid(x):
    return 1.0 / (1.0 + jnp.exp(-x))


def _rope_tables_kernel(pos_ref, par_ref, c_ref, s_ref):
    ang = pos_ref[...] * par_ref[0:1, :]
    c_ref[...] = jnp.where(par_ref[1:2, :] > 0.0, jnp.cos(ang), 1.0)
    s_ref[...] = jnp.sin(ang) * par_ref[2:3, :]


def _rope_tables(pos_col, rot_dim, period):
    t = pos_col.shape[0]
    half = rot_dim // 2
    freq = ROPE_THETA ** (-jnp.arange(half, dtype=F32) / half)
    lane = np.arange(LANES) % period
    in_rot = lane < rot_dim
    freq_lane = jnp.where(jnp.asarray(in_rot), freq[lane % half], 0.0)
    sign = np.where(lane < half, -1.0, np.where(in_rot, 1.0, 0.0))
    rows = [freq_lane, jnp.asarray(in_rot, F32), jnp.asarray(sign, F32)]
    par = jnp.stack(rows + [jnp.zeros((LANES,), F32)] * 5, axis=0)
    tm = min(t, 2048)
    spec = pl.BlockSpec((tm, LANES), lambda i: (i, 0))
    return pl.pallas_call(
        _rope_tables_kernel,
        grid=(t // tm,),
        in_specs=[pl.BlockSpec((tm, 1), lambda i: (i, 0)),
                  pl.BlockSpec((SUBLANES, LANES), lambda i: (0, 0))],
        out_specs=[spec, spec],
        out_shape=[jax.ShapeDtypeStruct((t, LANES), F32)] * 2,
        compiler_params=_params("parallel"),
        name="rope_tables",
    )(pos_col, par)


def _partner_matrix(head, half):
    m = np.zeros((EP_CHUNK, EP_CHUNK), np.float32)
    for base in range(0, EP_CHUNK, head):
        for i in range(half):
            m[base + i + half, base + i] = 1.0
            m[base + i, base + i + half] = 1.0
    return jnp.asarray(m, MXU_DTYPE)


def _normed(x_ref, g_ref):
    xf = x_ref[...]
    ms = jnp.mean(xf * xf, axis=-1, keepdims=True)
    return (xf * lax.rsqrt(ms + NORM_EPS) * g_ref[...]).astype(MXU_DTYPE)


def _norm_proj_kernel(x_ref, g_ref, w_ref, o_ref, h_ref):
    @pl.when(pl.program_id(1) == 0)
    def _():
        h_ref[...] = _normed(x_ref, g_ref)

    o_ref[...] = jnp.dot(h_ref[...], w_ref[...], preferred_element_type=F32).astype(o_ref.dtype)


def _norm_proj(x2d, gain, w, *, tn, out_dtype, name):
    t, d = x2d.shape
    n = w.shape[1]
    tm = min(t, 1024)
    return pl.pallas_call(
        _norm_proj_kernel,
        grid=(t // tm, n // tn),
        in_specs=[pl.BlockSpec((tm, d), lambda i, j: (i, 0)),
                  pl.BlockSpec((1, d), lambda i, j: (0, 0)),
                  pl.BlockSpec((d, tn), lambda i, j: (0, j))],
        out_specs=pl.BlockSpec((tm, tn), lambda i, j: (i, j)),
        out_shape=jax.ShapeDtypeStruct((t, n), out_dtype),
        scratch_shapes=[pltpu.VMEM((tm, d), MXU_DTYPE)],
        compiler_params=_params("parallel", "arbitrary"),
        name=name,
    )(x2d, gain, w)


NORM_ROPE, ROPE, PLAIN = "norm_rope", "rope", "plain"


def _norm_proj_heads_kernel(dil, kinds, x_ref, g_ref, w_ref, part_ref, c_ref, s_ref, hg_ref,
                            o_ref, y_buf):
    h = _normed(x_ref, g_ref)
    per_chunk = EP_CHUNK // LANES
    cos2 = jnp.concatenate([c_ref[...]] * per_chunk, axis=1)
    sin2 = jnp.concatenate([s_ref[...]] * per_chunk, axis=1)
    n = w_ref.shape[1]
    for g0 in range(0, n, PROJ_GROUP):
        width = min(PROJ_GROUP, n - g0)
        yg = jnp.dot(h, w_ref[:, g0:g0 + width], preferred_element_type=F32)
        for c0 in range(0, width, EP_CHUNK):
            kind = kinds[(g0 + c0) // EP_CHUNK]
            y = yg[:, c0:c0 + EP_CHUNK]
            if kind == NORM_ROPE:
                heads = []
                for k in range(EP_CHUNK // HEAD_DIM):
                    yh = y[:, k * HEAD_DIM:(k + 1) * HEAD_DIM]
                    ms = jnp.mean(yh * yh, axis=-1, keepdims=True)
                    lo = g0 + c0 + k * HEAD_DIM
                    heads.append(yh * lax.rsqrt(ms + NORM_EPS) * hg_ref[:, lo:lo + HEAD_DIM])
                y = jnp.concatenate(heads, axis=1)
            if kind != PLAIN:
                partner = jnp.dot(y.astype(MXU_DTYPE), part_ref[...], preferred_element_type=F32)
                y = y * cos2 + partner * sin2
            for k in range(per_chunk):
                y_buf[(g0 + c0) // LANES + k] = y[:, k * LANES:(k + 1) * LANES]

    rows = y_buf.shape[1] // dil
    for r in range(dil):
        for k in range(y_buf.shape[0]):
            slab = y_buf[k] if dil == 1 else y_buf[k, pl.ds(r, rows, stride=dil), :]
            o_ref[r, :, k * LANES:(k + 1) * LANES] = slab.astype(o_ref.dtype)


def _norm_proj_heads(x2d, gain, w, partner, tabs, head_gain, kinds, *, dil, name):
    t, d = x2d.shape
    n = w.shape[1]
    tm = min(t, 512)
    assert len(kinds) == n // EP_CHUNK
    tab = pl.BlockSpec((tm, LANES), lambda i: (i, 0))
    return pl.pallas_call(
        functools.partial(_norm_proj_heads_kernel, dil, tuple(kinds)),
        grid=(t // tm,),
        in_specs=[pl.BlockSpec((tm, d), lambda i: (i, 0)),
                  pl.BlockSpec((1, d), lambda i: (0, 0)),
                  pl.BlockSpec((d, n), lambda i: (0, 0)),
                  pl.BlockSpec((EP_CHUNK, EP_CHUNK), lambda i: (0, 0)),
                  tab, tab,
                  pl.BlockSpec((1, n), lambda i: (0, 0))],
        out_specs=pl.BlockSpec((dil, tm // dil, n), lambda i: (0, i, 0)),
        out_shape=jax.ShapeDtypeStruct((dil, t // dil, n), MXU_DTYPE),
        scratch_shapes=[pltpu.VMEM((n // LANES, tm, LANES), F32)],
        compiler_params=_params("parallel"),
        name=name,
    )(x2d, gain, w, partner, tabs[0], tabs[1], head_gain)


def _dilated_kernel(span, dil, q_ref, kc_ref, kp_ref, vc_ref, vp_ref, o_ref, lse_ref):
    i = pl.program_id(1)
    qi = lax.broadcasted_iota(jnp.int32, (A_BLOCK, 2 * A_BLOCK), 0) + A_BLOCK
    kj = lax.broadcasted_iota(jnp.int32, (A_BLOCK, 2 * A_BLOCK), 1)
    dist = qi - kj
    min_key = jnp.where(i > 0, 0, A_BLOCK)
    bias = jnp.where((dist >= 0) & (dist <= span) & (kj >= min_key), 0.0, NEG_BIG)

    def residue(r, carry):
        rows = pl.ds(r, A_BLOCK, stride=dil) if dil > 1 else slice(None)
        for h in range(A_HEADS_PER_GROUP):
            hs = slice(h * HEAD_DIM, (h + 1) * HEAD_DIM)
            q = q_ref[r, :, hs]
            k2 = jnp.concatenate([kp_ref[r, :, hs], kc_ref[r, :, hs]], axis=0)
            v2 = jnp.concatenate([vp_ref[r, :, hs], vc_ref[r, :, hs]], axis=0)
            s = lax.dot_general(q, k2, (((1,), (1,)), ((), ())), preferred_element_type=F32) + bias
            m = jnp.max(s, axis=-1, keepdims=True)
            p = jnp.exp(s - m)
            l = jnp.sum(p, axis=-1, keepdims=True)
            o = jnp.dot(p.astype(v2.dtype), v2, preferred_element_type=F32)
            o_ref[h, rows, :] = o / l
            lse_ref[h, rows, :] = jnp.broadcast_to(m + jnp.log(l), (A_BLOCK, HEAD_DIM))
        return carry

    lax.fori_loop(0, dil, residue, 0)


def _dilated_group(qkv, bsz, s, window, dil):
    n = s // dil
    nb = n // A_BLOCK
    blk = (dil, A_BLOCK, A_GROUP_WIDTH)

    def prev(i):
        return jnp.maximum(i - 1, 0)

    in_specs = [
        pl.BlockSpec(blk, lambda b, i: (0, b * nb + i, 0)),
        pl.BlockSpec(blk, lambda b, i: (0, b * nb + i, 1)),
        pl.BlockSpec(blk, lambda b, i: (0, b * nb + prev(i), 1)),
        pl.BlockSpec(blk, lambda b, i: (0, b * nb + i, 2)),
        pl.BlockSpec(blk, lambda b, i: (0, b * nb + prev(i), 2)),
    ]
    out_spec = pl.BlockSpec((A_HEADS_PER_GROUP, dil * A_BLOCK, HEAD_DIM), lambda b, i: (0, b * nb + i, 0))
    out_sds = jax.ShapeDtypeStruct((A_HEADS_PER_GROUP, bsz * s, HEAD_DIM), F32)
    return pl.pallas_call(
        functools.partial(_dilated_kernel, window // dil, dil),
        grid=(bsz, nb),
        in_specs=in_specs,
        out_specs=[out_spec, out_spec],
        out_shape=[out_sds, out_sds],
        compiler_params=_params("parallel", "arbitrary"),
        name=f"dilated_attn_d{dil}",
    )(qkv, qkv, qkv, qkv, qkv)


def _merge_groups_kernel(o0, o1, o2, l0, l1, l2, out_ref):
    for h in range(A_HEADS_PER_GROUP):
        a, b, c = l0[h], l1[h], l2[h]
        m = jnp.maximum(jnp.maximum(a, b), c)
        ea, eb, ec = jnp.exp(a - m), jnp.exp(b - m), jnp.exp(c - m)
        num = ea * o0[h] + eb * o1[h] + ec * o2[h]
        out_ref[:, h * HEAD_DIM:(h + 1) * HEAD_DIM] = (num / (ea + eb + ec)).astype(out_ref.dtype)


def _merge_groups(os_, lses):
    t = os_[0].shape[1]
    tm = min(t, 1024)
    spec = pl.BlockSpec((A_HEADS_PER_GROUP, tm, HEAD_DIM), lambda i: (0, i, 0))
    return pl.pallas_call(
        _merge_groups_kernel,
        grid=(t // tm,),
        in_specs=[spec] * 6,
        out_specs=pl.BlockSpec((tm, A_GROUP_WIDTH), lambda i: (i, 0)),
        out_shape=jax.ShapeDtypeStruct((t, A_GROUP_WIDTH), MXU_DTYPE),
        compiler_params=_params("parallel"),
        name="dilated_merge",
    )(*os_, *lses)


def _tree_row_sum(x, group):
    parts = [x[g * group:(g + 1) * group, :] for g in range(x.shape[0] // group)]
    while len(parts) > 1:
        parts = [a + b for a, b in zip(parts[0::2], parts[1::2])]
    return parts[0]


def _sparse_kernel(topk, idx_scale, qit_ref, wt_ref, ki_ref, qbt_ref, kb_ref, vt_ref,
                   out_ref, keys_ref, hi_ref, acc_ref, m_ref, s_idx0, s_idx1, s_att0, s_att1,
                   p0, p1, al0, al1, bias_buf):
    qb = pl.program_id(1)
    t0 = qb * QB_BLOCK
    n_pairs = (t0 + QB_BLOCK + 2 * KEY_TILE - 1) // (2 * KEY_TILE)
    n_tiles = 2 * n_pairs
    last_tile = n_tiles - 1
    lane_q = lax.broadcasted_iota(jnp.int32, (KEY_TILE, QB_BLOCK), 1) + t0
    row_id = lax.broadcasted_iota(jnp.int32, (KEY_TILE, QB_BLOCK), 0)
    query_pos = lax.broadcasted_iota(jnp.int32, (1, QB_BLOCK), 1) + t0
    few_keys = query_pos < topk
    wts = wt_ref[...] * idx_scale

    def tile_start(kt):
        return pl.multiple_of(kt * KEY_TILE, KEY_TILE)

    def run_pipelined(steps):
        def body(j, carry):
            steps(j * STEPS_PER_BODY, STEPS_PER_BODY)
            return carry

        n_bodies = n_tiles // STEPS_PER_BODY
        lax.fori_loop(0, n_bodies, body, 0)
        if STEPS_PER_BODY > 2:
            @pl.when(n_tiles % STEPS_PER_BODY != 0)
            def _():
                steps(n_bodies * STEPS_PER_BODY, 2)

    def idx_matmul(kt, buf):
        k_tile = ki_ref[pl.ds(tile_start(kt), KEY_TILE), 0:IDX_DIM]
        buf[...] = jnp.dot(k_tile, qit_ref[...], preferred_element_type=F32)

    def idx_reduce(kt, buf):
        r0 = tile_start(kt)
        acc = jnp.zeros((KEY_TILE, QB_BLOCK), F32)
        for h in range(IDX_HEADS):
            acc = acc + jnp.maximum(buf[:, h * QB_BLOCK:(h + 1) * QB_BLOCK], 0.0) * wts[h:h + 1, :]
        acc = jnp.where(jnp.abs(acc) < F32_MIN_NORMAL, 0.0, acc)
        acc = jnp.where(row_id + r0 <= lane_q, acc, -jnp.inf)
        bits = pltpu.bitcast(acc, jnp.int32)
        keys_ref[pl.ds(r0, KEY_TILE), :] = bits ^ ((bits >> 31) & 0x7FFFFFFF)
        hi_ref[pl.ds(r0, KEY_TILE), :] = pltpu.bitcast(bits & HI16_MASK, F32).astype(jnp.bfloat16)

    def score_steps(first, count):
        bufs = (s_idx0, s_idx1)
        for k in range(count):
            idx_matmul(jnp.minimum(first + k + 1, last_tile), bufs[(k + 1) % 2])
            idx_reduce(first + k, bufs[k % 2])

    idx_matmul(0, s_idx0)
    run_pipelined(score_steps)

    def sum_over_keys(partial, acc_rows, dtype):
        chunk_rows = COUNT_VREGS * acc_rows
        per_chunk = chunk_rows // KEY_TILE
        n_chunks = n_tiles // per_chunk

        def chunk(c, acc):
            return acc + partial(pl.multiple_of(c * chunk_rows, chunk_rows), chunk_rows)

        def tile(kt, acc):
            return acc + partial(tile_start(kt), KEY_TILE)

        acc = lax.fori_loop(0, n_chunks, chunk, jnp.zeros((acc_rows, QB_BLOCK), dtype))
        acc = lax.fori_loop(n_chunks * per_chunk, n_tiles, tile, acc)
        return jnp.sum(acc, axis=0, keepdims=True)

    def count_prefix_ge(prefix):
        prefix = jnp.where((prefix > 0) & (prefix < BF16_MIN_NORMAL), BF16_MIN_NORMAL, prefix)
        fbits = (prefix ^ ((prefix >> 15) & 0x7FFF)) << 16
        cand = pltpu.bitcast(fbits, F32).astype(jnp.bfloat16)

        def partial(r0, rows):
            hit = jnp.where(hi_ref[pl.ds(r0, rows), :] >= cand,
                            jnp.ones((rows, QB_BLOCK), jnp.bfloat16),
                            jnp.zeros((rows, QB_BLOCK), jnp.bfloat16))
            return _tree_row_sum(hit, BF16_ROWS).astype(F32)

        return sum_over_keys(partial, BF16_ROWS, F32)

    want = jnp.float32(topk)
    zero_prefix = jnp.zeros((1, QB_BLOCK), jnp.int32)
    cnt0 = count_prefix_ge(zero_prefix)
    take0 = cnt0 >= want
    prefix = jnp.where(take0, zero_prefix, -(2 ** 15))
    cnt = jnp.where(take0, cnt0, jnp.float32(2 ** 24))

    def prefix_step(i, carry):
        prefix, cnt = carry
        cand = prefix | lax.shift_left(jnp.int32(1), 14 - i)
        c = count_prefix_ge(cand)
        take = c >= want
        return jnp.where(take, cand, prefix), jnp.where(take, c, cnt)

    prefix, cnt = lax.fori_loop(0, 15, prefix_step, (prefix, cnt))

    def count_where(pred):
        def partial(r0, rows):
            rid = lax.broadcasted_iota(jnp.int32, (rows, QB_BLOCK), 0) + r0
            hit = jnp.where(pred(keys_ref[pl.ds(r0, rows), :], rid), 1, 0)
            return _tree_row_sum(hit, SUBLANES)
        return sum_over_keys(partial, SUBLANES, jnp.int32)

    def count_ge(cand):
        return count_where(lambda k, rid: k >= cand)

    def unresolved(cnt):
        return jnp.max(jnp.where(few_keys | (cnt == topk), 0, 1)) > 0

    def low_cond(carry):
        bit, _, cnt = carry
        return (bit >= 0) & unresolved(cnt)

    def low_step(carry):
        bit, tau, cnt = carry
        cand = tau | lax.shift_left(jnp.int32(1), bit)
        c = count_ge(cand)
        take = c >= topk
        return bit - 1, jnp.where(take, cand, tau), jnp.where(take, c, cnt)

    _, tau, n_ge = lax.while_loop(low_cond, low_step,
                                  (jnp.int32(15), prefix << 16, cnt.astype(jnp.int32)))
    tau = jnp.where(few_keys, KEY_NEG_INF + 1, tau)

    @pl.when(jnp.max(jnp.where(few_keys, 0, n_ge)) > topk)
    def _():
        need = topk - count_ge(tau + 1)

        def idx_step(i, last):
            cand = last | lax.shift_left(jnp.int32(1), 30 - i)
            below = count_where(lambda k, rid: (k == tau) & (rid < cand))
            return jnp.where(below < need, cand, last)

        last = lax.fori_loop(0, 31, idx_step, jnp.zeros((1, QB_BLOCK), jnp.int32))

        def demote(kt, carry):
            r0 = tile_start(kt)
            k = keys_ref[pl.ds(r0, KEY_TILE), :]
            drop = (k == tau) & (row_id + r0 > last)
            keys_ref[pl.ds(r0, KEY_TILE), :] = jnp.where(drop, tau - 1, k)
            return carry

        lax.fori_loop(0, n_tiles, demote, 0)

    def att_matmul(kt, s_buf):
        s = jnp.dot(kb_ref[pl.ds(tile_start(kt), KEY_TILE), :], qbt_ref[...],
                    preferred_element_type=F32)
        s_buf[...] = s.astype(s_buf.dtype)

    def softmax_tile(kt, s_buf, p_buf, al_buf):
        sel = keys_ref[pl.ds(tile_start(kt), KEY_TILE), :] >= tau
        bias_buf[...] = jnp.where(sel, 0.0, NEG_BIG).astype(bias_buf.dtype)
        for h in range(B_HEADS):
            hs = slice(h * QB_BLOCK, (h + 1) * QB_BLOCK)
            s = s_buf[:, hs] + bias_buf[...]
            m_old = m_ref[:, hs]
            m_new = jnp.maximum(m_old, jnp.max(s, axis=0, keepdims=True).astype(F32))
            m_ref[:, hs] = m_new
            al_buf[:, hs] = jnp.exp2(m_old - m_new)
            p_buf[:, hs] = jnp.exp2(s - m_new.astype(s.dtype)).astype(p_buf.dtype)

    def value_matmul(kt, p_buf, al_buf):
        pv = jnp.dot(vt_ref[:, pl.ds(tile_start(kt), KEY_TILE)], p_buf[...],
                     preferred_element_type=F32)
        acc_ref[...] = acc_ref[...] * al_buf[...] + pv

    m_ref[...] = jnp.full(m_ref.shape, NEG_BIG, F32)
    acc_ref[...] = jnp.zeros(acc_ref.shape, F32)
    p1[...] = jnp.zeros(p1.shape, p1.dtype)
    al1[...] = jnp.ones(al1.shape, F32)
    att_matmul(0, s_att0)

    def attend_steps(first, count):
        s_bufs, p_bufs, al_bufs = (s_att0, s_att1), (p0, p1), (al0, al1)
        for k in range(count):
            cur, oth = k % 2, (k + 1) % 2
            att_matmul(jnp.minimum(first + k + 1, last_tile), s_bufs[oth])
            value_matmul(jnp.maximum(first + k - 1, 0), p_bufs[oth], al_bufs[oth])
            softmax_tile(first + k, s_bufs[cur], p_bufs[cur], al_bufs[cur])

    run_pipelined(attend_steps)
    value_matmul(last_tile, p1, al1)

    for h in range(B_HEADS):
        hs = slice(h * QB_BLOCK, (h + 1) * QB_BLOCK)
        o_t = acc_ref[0:HEAD_DIM, hs] / acc_ref[HEAD_DIM:HEAD_DIM + 1, hs]
        out_ref[:, h * HEAD_DIM:(h + 1) * HEAD_DIM] = o_t.T.astype(out_ref.dtype)


def _sparse_mixer(bproj, idx, wi_t, bsz, s):
    nqb = s // QB_BLOCK
    topk = min(B_TOPK_MAX, s // 4)
    idx_scale = (IDX_HEADS ** -0.5) * (IDX_DIM ** -0.5)
    nq = B_HEADS * QB_BLOCK
    qi = idx[:, :IDX_HEADS * IDX_DIM].reshape(bsz, nqb, QB_BLOCK, IDX_HEADS, IDX_DIM)
    qit = qi.transpose(0, 1, 4, 3, 2).reshape(bsz, nqb, IDX_DIM, IDX_HEADS * QB_BLOCK)
    qbq = bproj[:, :B_HEADS * HEAD_DIM].reshape(bsz, nqb, QB_BLOCK, B_HEADS, HEAD_DIM)
    qbt = qbq.transpose(0, 1, 4, 3, 2).reshape(bsz, nqb, HEAD_DIM, nq)
    vb = bproj[:, VB_COL:VB_COL + HEAD_DIM].reshape(bsz, s, HEAD_DIM)
    vt = jnp.concatenate([vb.transpose(0, 2, 1),
                          jnp.ones((bsz, 1, s), vb.dtype),
                          jnp.zeros((bsz, SUBLANES - 1, s), vb.dtype)], axis=1)
    v_rows = HEAD_DIM + SUBLANES
    b3 = bproj.reshape(bsz, s, B_COLS)
    idx3 = idx.reshape(bsz, s, IDX_COLS)
    kb_blk = KB_COL // LANES
    ki_blk = (IDX_HEADS * IDX_DIM) // LANES
    out = pl.pallas_call(
        functools.partial(_sparse_kernel, topk, idx_scale),
        grid=(bsz, nqb),
        in_specs=[
            pl.BlockSpec((None, None, IDX_DIM, IDX_HEADS * QB_BLOCK), lambda b, q: (b, q, 0, 0)),
            pl.BlockSpec((None, None, IDX_HEADS, QB_BLOCK), lambda b, q: (b, q, 0, 0)),
            pl.BlockSpec((None, s, LANES), lambda b, q: (b, 0, ki_blk)),
            pl.BlockSpec((None, None, HEAD_DIM, nq), lambda b, q: (b, q, 0, 0)),
            pl.BlockSpec((None, s, LANES), lambda b, q: (b, 0, kb_blk)),
            pl.BlockSpec((None, v_rows, s), lambda b, q: (b, 0, 0)),
        ],
        out_specs=pl.BlockSpec((None, QB_BLOCK, B_HEADS * HEAD_DIM), lambda b, q: (b, q, 0)),
        out_shape=jax.ShapeDtypeStruct((bsz, s, B_HEADS * HEAD_DIM), MXU_DTYPE),
        scratch_shapes=[pltpu.VMEM((s, QB_BLOCK), jnp.int32),
                        pltpu.VMEM((s, QB_BLOCK), jnp.bfloat16),
                        pltpu.VMEM((v_rows, nq), F32),
                        pltpu.VMEM((1, nq), F32),
                        pltpu.VMEM((KEY_TILE, IDX_HEADS * QB_BLOCK), F32),
                        pltpu.VMEM((KEY_TILE, IDX_HEADS * QB_BLOCK), F32),
                        pltpu.VMEM((KEY_TILE, nq), SOFTMAX_DTYPE),
                        pltpu.VMEM((KEY_TILE, nq), SOFTMAX_DTYPE),
                        pltpu.VMEM((KEY_TILE, nq), MXU_DTYPE),
                        pltpu.VMEM((KEY_TILE, nq), MXU_DTYPE),
                        pltpu.VMEM((1, nq), F32),
                        pltpu.VMEM((1, nq), F32),
                        pltpu.VMEM((KEY_TILE, QB_BLOCK), SOFTMAX_DTYPE)],
        compiler_params=_params("parallel", "arbitrary"),
        name="sparse_mixer",
    )(qit, wi_t, idx3, qbt, b3, vt)
    return out.reshape(bsz * s, B_HEADS * HEAD_DIM)


def _lru_kernel(ts, xc_ref, yc_ref, cw_ref, cb_ref, wa_ref, wx_ref, ba_ref, bx_ref, lam_ref,
                o_ref, xbuf, a_buf, b_buf, h_state):
    halo = SUBLANES

    @pl.when(pl.program_id(1) == 0)
    def _():
        xbuf[0:halo, :] = jnp.zeros((halo, xbuf.shape[1]), F32)
        h_state[...] = jnp.zeros(h_state.shape, F32)

    xbuf[halo:halo + ts, :] = xc_ref[...]
    xc = cb_ref[...] + cw_ref[0:1, :] * xbuf[halo - 3:halo - 3 + ts, :]
    for tap in range(1, LRU_CONV_W):
        lo = halo - (LRU_CONV_W - 1) + tap
        xc = xc + cw_ref[tap:tap + 1, :] * xbuf[lo:lo + ts, :]
    xbuf[0:halo, :] = xc_ref[ts - halo:ts, :]

    z = -lam_ref[...]
    softplus = jnp.maximum(z, 0.0) + jnp.log1p(jnp.exp(-jnp.abs(z)))
    bw = xc.shape[1] // LRU_BLOCKS
    for blk in range(LRU_BLOCKS):
        cs = slice(blk * bw, (blk + 1) * bw)
        xs = xc[:, cs]
        xs_m = xs.astype(MXU_DTYPE)
        r = _sigmoid(jnp.dot(xs_m, wa_ref[blk], preferred_element_type=F32) + ba_ref[:, cs])
        gate_i = _sigmoid(jnp.dot(xs_m, wx_ref[blk], preferred_element_type=F32) + bx_ref[:, cs])
        log_a = (-LRU_C) * r * softplus[:, cs]
        a = jnp.exp(log_a)
        a_buf[:, cs] = a
        b_buf[:, cs] = jnp.sqrt(-jnp.tanh(log_a) * (a * a + 1.0)) * gate_i * xs

    def step(t, h):
        h = a_buf[pl.ds(t, 1), :] * h + b_buf[pl.ds(t, 1), :]
        b_buf[pl.ds(t, 1), :] = h
        return h

    h_state[0:1, :] = lax.fori_loop(0, ts, step, h_state[0:1, :], unroll=8)
    o_ref[...] = (b_buf[...] * _gelu_tanh(yc_ref[...])).astype(o_ref.dtype)


def _lru_mixer(feat, bsz, s, width, conv_w, conv_b, wa, ba, wx, bx, lam):
    ts = min(s, 512)
    nt = s // ts
    row = lambda a: a.reshape(1, width).astype(F32)
    vec = pl.BlockSpec((1, width), lambda b, t: (0, 0))
    wspec = pl.BlockSpec(wa.shape, lambda b, t: (0, 0, 0))
    return pl.pallas_call(
        functools.partial(_lru_kernel, ts),
        grid=(bsz, nt),
        in_specs=[pl.BlockSpec((ts, width), lambda b, t: (b * nt + t, 0)),
                  pl.BlockSpec((ts, width), lambda b, t: (b * nt + t, 1)),
                  pl.BlockSpec((LRU_CONV_W, width), lambda b, t: (0, 0)),
                  vec, wspec, wspec, vec, vec, vec],
        out_specs=pl.BlockSpec((ts, width), lambda b, t: (b * nt + t, 0)),
        out_shape=jax.ShapeDtypeStruct((bsz * s, width), MXU_DTYPE),
        scratch_shapes=[pltpu.VMEM((ts + SUBLANES, width), F32),
                        pltpu.VMEM((ts, width), F32),
                        pltpu.VMEM((ts, width), F32),
                        pltpu.VMEM((SUBLANES, width), F32)],
        compiler_params=_params("parallel", "arbitrary"),
        name="rglru_mixer",
    )(feat, feat, conv_w.astype(F32), row(conv_b), wa.astype(MXU_DTYPE), wx.astype(MXU_DTYPE),
      row(ba), row(bx), row(lam))


def _gate_merge_kernel(oa, ob, oc, wa, wb, wc, g0, g1, g2, b0, b1, b2, out_ref):
    acc = _sigmoid(g0[...] + b0[...]) * jnp.dot(oa[...], wa[...], preferred_element_type=F32)
    acc = acc + _sigmoid(g1[...] + b1[...]) * jnp.dot(ob[...], wb[...], preferred_element_type=F32)
    acc = acc + _sigmoid(g2[...] + b2[...]) * jnp.dot(oc[...], wc[...], preferred_element_type=F32)
    out_ref[...] = acc.astype(out_ref.dtype)


def _gate_merge(out_a, out_b, out_c, wpa, wpb, wpc, feat, gate_col0, b_gate_row, d_model):
    t = out_a.shape[0]
    tm = min(t, 1024)
    tn = 512
    nj = d_model // tn
    g_blk = gate_col0 // tn

    def act(a):
        return pl.BlockSpec((tm, a.shape[1]), lambda i, j: (i, 0))

    def wgt(w):
        return pl.BlockSpec((w.shape[0], tn), lambda i, j: (0, j))

    gates = [pl.BlockSpec((tm, tn), functools.partial(lambda br, i, j: (i, g_blk + br * nj + j), br))
             for br in range(N_BRANCH)]
    biases = [pl.BlockSpec((1, tn), functools.partial(lambda br, i, j: (0, br * nj + j), br))
              for br in range(N_BRANCH)]
    return pl.pallas_call(
        _gate_merge_kernel,
        grid=(t // tm, nj),
        in_specs=[act(out_a), act(out_b), act(out_c), wgt(wpa), wgt(wpb), wgt(wpc)] + gates + biases,
        out_specs=pl.BlockSpec((tm, tn), lambda i, j: (i, j)),
        out_shape=jax.ShapeDtypeStruct((t, d_model), MXU_DTYPE),
        compiler_params=_params("parallel", "arbitrary"),
        name="gate_merge",
    )(out_a, out_b, out_c, wpa, wpb, wpc, feat, feat, feat, b_gate_row, b_gate_row, b_gate_row)


def _matmul_residual_kernel(a_ref, w_ref, r_ref, o_ref):
    o_ref[...] = r_ref[...] + jnp.dot(a_ref[...], w_ref[...], preferred_element_type=F32)


def _matmul_residual(a, w, resid, name):
    t, k = a.shape
    n = w.shape[1]
    tm = min(t, 1024)
    tn = 512
    return pl.pallas_call(
        _matmul_residual_kernel,
        grid=(t // tm, n // tn),
        in_specs=[pl.BlockSpec((tm, k), lambda i, j: (i, 0)),
                  pl.BlockSpec((k, tn), lambda i, j: (0, j)),
                  pl.BlockSpec((tm, tn), lambda i, j: (i, j))],
        out_specs=pl.BlockSpec((tm, tn), lambda i, j: (i, j)),
        out_shape=jax.ShapeDtypeStruct((t, n), F32),
        compiler_params=_params("parallel", "arbitrary"),
        name=name,
    )(a, w, resid)


def _ffn_up_kernel(tm, tiles_per_seq, x_ref, xh_ref, g_ref, wa_ref, wb_ref, cwa_ref, cwb_ref,
                   cba_ref, cbb_ref, o_ref, h_ref, hh_ref, ua_buf, ub_buf):
    halo = SUBLANES

    def normed(ref):
        xf = ref[...]
        ms = jnp.mean(xf * xf, axis=-1, keepdims=True)
        return (xf * lax.rsqrt(ms + NORM_EPS) * g_ref[...]).astype(MXU_DTYPE)

    @pl.when(pl.program_id(1) == 0)
    def _():
        h_ref[...] = normed(x_ref)
        hh_ref[...] = normed(xh_ref)

    keep = jnp.where(pl.program_id(0) % tiles_per_seq == 0, 0.0, 1.0)

    def conv_half(w_ref, cw_ref, cb_ref, buf):
        buf[0:halo, :] = keep * jnp.dot(hh_ref[...], w_ref[...], preferred_element_type=F32)
        buf[halo:halo + tm, :] = jnp.dot(h_ref[...], w_ref[...], preferred_element_type=F32)
        y = cb_ref[...] + cw_ref[0:1, :] * buf[halo - 2:halo - 2 + tm, :]
        for tap in range(1, FFN_CONV_W):
            lo = halo - (FFN_CONV_W - 1) + tap
            y = y + cw_ref[tap:tap + 1, :] * buf[lo:lo + tm, :]
        return y

    ya = conv_half(wa_ref, cwa_ref, cba_ref, ua_buf)
    yb = conv_half(wb_ref, cwb_ref, cbb_ref, ub_buf)
    o_ref[...] = (_gelu_tanh(ya) * yb).astype(o_ref.dtype)


def _ffn_up(x2d, s, gain, w_up, conv_w, conv_b_row):
    t, d = x2d.shape
    d_ff = w_up.shape[1] // 2
    tm = min(s, 1024)
    tn = 512
    nj = d_ff // tn
    hb = tm // SUBLANES
    return pl.pallas_call(
        functools.partial(_ffn_up_kernel, tm, s // tm),
        grid=(t // tm, nj),
        in_specs=[pl.BlockSpec((tm, d), lambda i, j: (i, 0)),
                  pl.BlockSpec((SUBLANES, d), lambda i, j: (jnp.maximum(i * hb - 1, 0), 0)),
                  pl.BlockSpec((1, d), lambda i, j: (0, 0)),
                  pl.BlockSpec((d, tn), lambda i, j: (0, j)),
                  pl.BlockSpec((d, tn), lambda i, j: (0, nj + j)),
                  pl.BlockSpec((FFN_CONV_W, tn), lambda i, j: (0, j)),
                  pl.BlockSpec((FFN_CONV_W, tn), lambda i, j: (0, nj + j)),
                  pl.BlockSpec((1, tn), lambda i, j: (0, j)),
                  pl.BlockSpec((1, tn), lambda i, j: (0, nj + j))],
        out_specs=pl.BlockSpec((tm, tn), lambda i, j: (i, j)),
        out_shape=jax.ShapeDtypeStruct((t, d_ff), MXU_DTYPE),
        scratch_shapes=[pltpu.VMEM((tm, d), MXU_DTYPE),
                        pltpu.VMEM((SUBLANES, d), MXU_DTYPE),
                        pltpu.VMEM((tm + SUBLANES, tn), F32),
                        pltpu.VMEM((tm + SUBLANES, tn), F32)],
        compiler_params=_params("parallel", "arbitrary"),
        name="ffn_up_conv_gate",
    )(x2d, x2d, gain, w_up, w_up, conv_w, conv_w, conv_b_row, conv_b_row)


def _pad_cols(parts, total, dtype):
    width = sum(p.shape[1] for p in parts)
    rows = parts[0].shape[0]
    if total > width:
        parts = list(parts) + [jnp.zeros((rows, total - width), parts[0].dtype)]
    return jnp.concatenate(parts, axis=1).astype(dtype)


def _temporal_block(x2d, bsz, s, tabs_head, tabs_idx, w_in, b_gate, qn_a, kn_a, qn_b, kn_b,
                    lru_conv_w, lru_conv_b, lru_wa, lru_ba, lru_wx, lru_bx, lru_lambda,
                    w_proj_a, w_proj_b, w_proj_c, w_out, norm_mix):
    d = x2d.shape[1]
    a_w = A_HEADS * HEAD_DIM
    b_w = B_HEADS * HEAD_DIM
    sizes = (a_w, a_w, a_w, b_w, HEAD_DIM, HEAD_DIM, IDX_HEADS * IDX_DIM, IDX_DIM, IDX_HEADS,
             d, d, N_BRANCH * d)
    offs = np.concatenate([[0], np.cumsum(sizes)])
    col = lambda k: w_in[:, offs[k]:offs[k + 1]]
    w_qa, w_ka, w_va, w_qb, w_kb, w_vb, w_qi, w_ki, w_wi, w_xc, w_yc, w_g = [col(k) for k in range(12)]

    gain_row = norm_mix.reshape(1, d).astype(F32)
    att_scale = HEAD_DIM ** -0.5
    ones = lambda k: jnp.ones((k,), F32)
    part_head = _partner_matrix(HEAD_DIM, ROT_DIM // 2)
    part_idx = _partner_matrix(IDX_DIM, IDX_ROT // 2)
    gw = A_GROUP_WIDTH

    a_gain = jnp.concatenate([jnp.tile(qn_a.astype(F32) * att_scale, A_HEADS_PER_GROUP),
                              jnp.tile(kn_a.astype(F32), A_HEADS_PER_GROUP), ones(gw)]).reshape(1, 3 * gw)
    a_kinds = [NORM_ROPE] * 4 + [PLAIN] * 2
    os_, lses = [], []
    for g, (window, dil) in enumerate(DIL_GROUPS):
        gs = slice(g * gw, (g + 1) * gw)
        w_g3 = jnp.concatenate([w_qa[:, gs], w_ka[:, gs], w_va[:, gs]], axis=1).astype(MXU_DTYPE)
        qkv = _norm_proj_heads(x2d, gain_row, w_g3, part_head, tabs_head, a_gain, a_kinds,
                               dil=dil, name=f"proj_a_d{dil}")
        o, lse = _dilated_group(qkv, bsz, s, window, dil)
        os_.append(o)
        lses.append(lse)
    out_a = _merge_groups(os_, lses)

    b_gain = jnp.concatenate([jnp.tile(qn_b.astype(F32) * (att_scale * LOG2_E), B_HEADS),
                              kn_b.astype(F32), ones(B_COLS - b_w - HEAD_DIM)]).reshape(1, B_COLS)
    zeros = jnp.zeros((d, HEAD_DIM), w_in.dtype)
    w_b = jnp.concatenate([w_qb, w_kb, zeros, w_vb, zeros], axis=1).astype(MXU_DTYPE)
    b_kinds = [NORM_ROPE] * (b_w // EP_CHUNK + 1) + [PLAIN]
    bproj = _norm_proj_heads(x2d, gain_row, w_b, part_head, tabs_head, b_gain, b_kinds,
                             dil=1, name="proj_b").reshape(bsz * s, B_COLS)
    idx = _norm_proj_heads(x2d, gain_row, _pad_cols([w_qi, w_ki], IDX_COLS, MXU_DTYPE), part_idx,
                           tabs_idx, jnp.ones((1, IDX_COLS), F32), [ROPE] * (IDX_COLS // EP_CHUNK),
                           dil=1, name="proj_idx").reshape(bsz * s, IDX_COLS)
    gate_col0 = 2 * d
    wi_col0 = gate_col0 + N_BRANCH * d
    feat_cols = -(-(wi_col0 + IDX_HEADS) // FEAT_TILE) * FEAT_TILE
    feat = _norm_proj(x2d, gain_row, _pad_cols([w_xc, w_yc, w_g, w_wi], feat_cols, MXU_DTYPE),
                      tn=FEAT_TILE, out_dtype=F32, name="proj_feat")
    wi = feat[:, wi_col0:wi_col0 + IDX_HEADS].reshape(bsz, s // QB_BLOCK, QB_BLOCK, IDX_HEADS)
    out_b = _sparse_mixer(bproj, idx, wi.transpose(0, 1, 3, 2), bsz, s)

    out_c = _lru_mixer(feat, bsz, s, d, lru_conv_w, lru_conv_b, lru_wa, lru_ba, lru_wx, lru_bx,
                       lru_lambda)

    merged = _gate_merge(out_a, out_b, out_c, w_proj_a.astype(MXU_DTYPE),
                         w_proj_b.astype(MXU_DTYPE), w_proj_c.astype(MXU_DTYPE), feat, gate_col0,
                         b_gate.reshape(1, N_BRANCH * d).astype(F32), d)
    return _matmul_residual(merged, w_out.astype(MXU_DTYPE), x2d, "out_proj_residual")


def _conv_ffn_block(x2d, s, norm_ffn, ffn_w_up, ffn_conv_w, ffn_conv_b, ffn_w_down):
    d = x2d.shape[1]
    act = _ffn_up(x2d, s, norm_ffn.reshape(1, d).astype(F32), ffn_w_up.astype(MXU_DTYPE),
                  ffn_conv_w.astype(F32), ffn_conv_b.reshape(1, -1).astype(F32))
    return _matmul_residual(act, ffn_w_down.astype(MXU_DTYPE), x2d, "ffn_down_residual")


def kernel(x, positions, w_in, b_gate, qn_a, kn_a, qn_b, kn_b, lru_conv_w, lru_conv_b, lru_wa,
           lru_ba, lru_wx, lru_bx, lru_lambda, w_proj_a, w_proj_b, w_proj_c, w_out, norm_mix,
           norm_ffn, ffn_w_up, ffn_conv_w, ffn_conv_b, ffn_w_down):
    bsz, s, d = x.shape
    assert s % (DIL_GROUPS[-1][1] * A_BLOCK) == 0 and s % (2 * KEY_TILE) == 0
    x2d = x.reshape(bsz * s, d)
    pos_col = positions.astype(F32).reshape(bsz * s, 1)
    tabs_head = _rope_tables(pos_col, ROT_DIM, HEAD_DIM)
    tabs_idx = _rope_tables(pos_col, IDX_ROT, IDX_DIM)
    for layer in range(w_in.shape[0]):
        x2d = _temporal_block(
            x2d, bsz, s, tabs_head, tabs_idx, w_in[layer], b_gate[layer], qn_a[layer],
            kn_a[layer], qn_b[layer], kn_b[layer], lru_conv_w[layer], lru_conv_b[layer],
            lru_wa[layer], lru_ba[layer], lru_wx[layer], lru_bx[layer], lru_lambda[layer],
            w_proj_a[layer], w_proj_b[layer], w_proj_c[layer], w_out[layer], norm_mix[layer])
        x2d = _conv_ffn_block(x2d, s, norm_ffn[layer], ffn_w_up[layer], ffn_conv_w[layer],
                              ffn_conv_b[layer], ffn_w_down[layer])
    return x2d.reshape(bsz, s, d)
```

```python
import functools
import math

import numpy as np
import jax
import jax.numpy as jnp
from jax import lax
from jax.experimental import pallas as pl
from jax.experimental.pallas import tpu as pltpu

F32 = jnp.float32
MXU_DTYPE = jnp.bfloat16
SOFTMAX_DTYPE = jnp.float32

LANES = 128
SUBLANES = 8
VMEM_LIMIT_BYTES = 58 * 1024 * 1024

HEAD_DIM = 128
ROPE_THETA = 500000.0
ROT_DIM = HEAD_DIM // 4
NORM_EPS = 1e-6
DIL_GROUPS = ((128, 1), (512, 4), (2048, 16))
A_HEADS_PER_GROUP = 4
A_HEADS = A_HEADS_PER_GROUP * len(DIL_GROUPS)
A_BLOCK = 128
B_HEADS = 12
B_TOPK_MAX = 256
IDX_HEADS = 16
IDX_DIM = 64
IDX_ROT = IDX_DIM // 4
LRU_BLOCKS = 16
LRU_C = 8.0
LRU_CONV_W = 4
FFN_CONV_W = 3
N_BRANCH = 3
A_GROUP_WIDTH = A_HEADS_PER_GROUP * HEAD_DIM

B_COLS = 2048
KB_COL = 1536
VB_COL = 1792
IDX_COLS = 1280
EP_CHUNK = 256
PROJ_GROUP = 512
FEAT_TILE = 1536
QB_BLOCK = 128
KEY_TILE = 256
STEPS_PER_BODY = 4
INT_MIN = -(2 ** 31)
KEY_NEG_INF = INT_MIN + 0x7FFFFF
HI16_MASK = -65536
BF16_ROWS = 16
COUNT_VREGS = 64
BF16_MIN_NORMAL = 0x80
F32_MIN_NORMAL = 1.1754943508222875e-38
NEG_BIG = -(2.0 ** 100)
LOG2_E = 1.4426950408889634


def _params(*semantics):
    return pltpu.CompilerParams(dimension_semantics=semantics,
                                vmem_limit_bytes=VMEM_LIMIT_BYTES)


def _gelu_tanh(x):
    return 0.5 * x * (1.0 + jnp.tanh(0.7978845608028654 * (x + 0.044715 * (x * x * x))))


def _sigmoid(x):
    return 1.0 / (1.0 + jnp.exp(-x))


def _rope_tables_kernel(pos_ref, par_ref, c_ref, s_ref):
    ang = pos_ref[...] * par_ref[0:1, :]
    c_ref[...] = jnp.where(par_ref[1:2, :] > 0.0, jnp.cos(ang), 1.0)
    s_ref[...] = jnp.sin(ang) * par_ref[2:3, :]


def _rope_tables(pos_col, rot_dim, period):
    t = pos_col.shape[0]
    half = rot_dim // 2
    freq = ROPE_THETA ** (-jnp.arange(half, dtype=F32) / half)
    lane = np.arange(LANES) % period
    in_rot = lane < rot_dim
    freq_lane = jnp.where(jnp.asarray(in_rot), freq[lane % half], 0.0)
    sign = np.where(lane < half, -1.0, np.where(in_rot, 1.0, 0.0))
    rows = [freq_lane, jnp.asarray(in_rot, F32), jnp.asarray(sign, F32)]
    par = jnp.stack(rows + [jnp.zeros((LANES,), F32)] * 5, axis=0)
    tm = min(t, 2048)
    spec = pl.BlockSpec((tm, LANES), lambda i: (i, 0))
    return pl.pallas_call(
        _rope_tables_kernel,
        grid=(t // tm,),
        in_specs=[pl.BlockSpec((tm, 1), lambda i: (i, 0)),
                  pl.BlockSpec((SUBLANES, LANES), lambda i: (0, 0))],
        out_specs=[spec, spec],
        out_shape=[jax.ShapeDtypeStruct((t, LANES), F32)] * 2,
        compiler_params=_params("parallel"),
        name="rope_tables",
    )(pos_col, par)


def _partner_matrix(head, half):
    m = np.zeros((EP_CHUNK, EP_CHUNK), np.float32)
    for base in range(0, EP_CHUNK, head):
        for i in range(half):
            m[base + i + half, base + i] = 1.0
            m[base + i, base + i + half] = 1.0
    return jnp.asarray(m, MXU_DTYPE)


def _normed(x_ref, g_ref):
    xf = x_ref[...]
    ms = jnp.mean(xf * xf, axis=-1, keepdims=True)
    return (xf * lax.rsqrt(ms + NORM_EPS) * g_ref[...]).astype(MXU_DTYPE)


def _norm_proj_kernel(x_ref, g_ref, w_ref, o_ref, h_ref):
    @pl.when(pl.program_id(1) == 0)
    def _():
        h_ref[...] = _normed(x_ref, g_ref)

    o_ref[...] = jnp.dot(h_ref[...], w_ref[...], preferred_element_type=F32).astype(o_ref.dtype)


def _norm_proj(x2d, gain, w, *, tn, out_dtype, name):
    t, d = x2d.shape
    n = w.shape[1]
    tm = min(t, 1024)
    return pl.pallas_call(
        _norm_proj_kernel,
        grid=(t // tm, n // tn),
        in_specs=[pl.BlockSpec((tm, d), lambda i, j: (i, 0)),
                  pl.BlockSpec((1, d), lambda i, j: (0, 0)),
                  pl.BlockSpec((d, tn), lambda i, j: (0, j))],
        out_specs=pl.BlockSpec((tm, tn), lambda i, j: (i, j)),
        out_shape=jax.ShapeDtypeStruct((t, n), out_dtype),
        scratch_shapes=[pltpu.VMEM((tm, d), MXU_DTYPE)],
        compiler_params=_params("parallel", "arbitrary"),
        name=name,
    )(x2d, gain, w)


NORM_ROPE, ROPE, PLAIN = "norm_rope", "rope", "plain"


def _norm_proj_heads_kernel(dil, kinds, x_ref, g_ref, w_ref, part_ref, c_ref, s_ref, hg_ref,
                            o_ref, y_buf):
    h = _normed(x_ref, g_ref)
    per_chunk = EP_CHUNK // LANES
    cos2 = jnp.concatenate([c_ref[...]] * per_chunk, axis=1)
    sin2 = jnp.concatenate([s_ref[...]] * per_chunk, axis=1)
    n = w_ref.shape[1]
    for g0 in range(0, n, PROJ_GROUP):
        width = min(PROJ_GROUP, n - g0)
        yg = jnp.dot(h, w_ref[:, g0:g0 + width], preferred_element_type=F32)
        for c0 in range(0, width, EP_CHUNK):
            kind = kinds[(g0 + c0) // EP_CHUNK]
            y = yg[:, c0:c0 + EP_CHUNK]
            if kind == NORM_ROPE:
                heads = []
                for k in range(EP_CHUNK // HEAD_DIM):
                    yh = y[:, k * HEAD_DIM:(k + 1) * HEAD_DIM]
                    ms = jnp.mean(yh * yh, axis=-1, keepdims=True)
                    lo = g0 + c0 + k * HEAD_DIM
                    heads.append(yh * lax.rsqrt(ms + NORM_EPS) * hg_ref[:, lo:lo + HEAD_DIM])
                y = jnp.concatenate(heads, axis=1)
            if kind != PLAIN:
                partner = jnp.dot(y.astype(MXU_DTYPE), part_ref[...], preferred_element_type=F32)
                y = y * cos2 + partner * sin2
            for k in range(per_chunk):
                y_buf[(g0 + c0) // LANES + k] = y[:, k * LANES:(k + 1) * LANES]

    rows = y_buf.shape[1] // dil
    for r in range(dil):
        for k in range(y_buf.shape[0]):
            slab = y_buf[k] if dil == 1 else y_buf[k, pl.ds(r, rows, stride=dil), :]
            o_ref[r, :, k * LANES:(k + 1) * LANES] = slab.astype(o_ref.dtype)


def _norm_proj_heads(x2d, gain, w, partner, tabs, head_gain, kinds, *, dil, name):
    t, d = x2d.shape
    n = w.shape[1]
    tm = min(t, 512)
    assert len(kinds) == n // EP_CHUNK
    tab = pl.BlockSpec((tm, LANES), lambda i: (i, 0))
    return pl.pallas_call(
        functools.partial(_norm_proj_heads_kernel, dil, tuple(kinds)),
        grid=(t // tm,),
        in_specs=[pl.BlockSpec((tm, d), lambda i: (i, 0)),
                  pl.BlockSpec((1, d), lambda i: (0, 0)),
                  pl.BlockSpec((d, n), lambda i: (0, 0)),
                  pl.BlockSpec((EP_CHUNK, EP_CHUNK), lambda i: (0, 0)),
                  tab, tab,
                  pl.BlockSpec((1, n), lambda i: (0, 0))],
        out_specs=pl.BlockSpec((dil, tm // dil, n), lambda i: (0, i, 0)),
        out_shape=jax.ShapeDtypeStruct((dil, t // dil, n), MXU_DTYPE),
        scratch_shapes=[pltpu.VMEM((n // LANES, tm, LANES), F32)],
        compiler_params=_params("parallel"),
        name=name,
    )(x2d, gain, w, partner, tabs[0], tabs[1], head_gain)


def _dilated_kernel(span, dil, q_ref, kc_ref, kp_ref, vc_ref, vp_ref, o_ref, lse_ref):
    i = pl.program_id(1)
    qi = lax.broadcasted_iota(jnp.int32, (A_BLOCK, 2 * A_BLOCK), 0) + A_BLOCK
    kj = lax.broadcasted_iota(jnp.int32, (A_BLOCK, 2 * A_BLOCK), 1)
    dist = qi - kj
    min_key = jnp.where(i > 0, 0, A_BLOCK)
    bias = jnp.where((dist >= 0) & (dist <= span) & (kj >= min_key), 0.0, NEG_BIG)

    def residue(r, carry):
        rows = pl.ds(r, A_BLOCK, stride=dil) if dil > 1 else slice(None)
        for h in range(A_HEADS_PER_GROUP):
            hs = slice(h * HEAD_DIM, (h + 1) * HEAD_DIM)
            q = q_ref[r, :, hs]
            k2 = jnp.concatenate([kp_ref[r, :, hs], kc_ref[r, :, hs]], axis=0)
            v2 = jnp.concatenate([vp_ref[r, :, hs], vc_ref[r, :, hs]], axis=0)
            s = lax.dot_general(q, k2, (((1,), (1,)), ((), ())), preferred_element_type=F32) + bias
            m = jnp.max(s, axis=-1, keepdims=True)
            p = jnp.exp(s - m)
            l = jnp.sum(p, axis=-1, keepdims=True)
            o = jnp.dot(p.astype(v2.dtype), v2, preferred_element_type=F32)
            o_ref[h, rows, :] = o / l
            lse_ref[h, rows, :] = jnp.broadcast_to(m + jnp.log(l), (A_BLOCK, HEAD_DIM))
        return carry

    lax.fori_loop(0, dil, residue, 0)


def _dilated_group(qkv, bsz, s, window, dil):
    n = s // dil
    nb = n // A_BLOCK
    blk = (dil, A_BLOCK, A_GROUP_WIDTH)

    def prev(i):
        return jnp.maximum(i - 1, 0)

    in_specs = [
        pl.BlockSpec(blk, lambda b, i: (0, b * nb + i, 0)),
        pl.BlockSpec(blk, lambda b, i: (0, b * nb + i, 1)),
        pl.BlockSpec(blk, lambda b, i: (0, b * nb + prev(i), 1)),
        pl.BlockSpec(blk, lambda b, i: (0, b * nb + i, 2)),
        pl.BlockSpec(blk, lambda b, i: (0, b * nb + prev(i), 2)),
    ]
    out_spec = pl.BlockSpec((A_HEADS_PER_GROUP, dil * A_BLOCK, HEAD_DIM), lambda b, i: (0, b * nb + i, 0))
    out_sds = jax.ShapeDtypeStruct((A_HEADS_PER_GROUP, bsz * s, HEAD_DIM), F32)
    return pl.pallas_call(
        functools.partial(_dilated_kernel, window // dil, dil),
        grid=(bsz, nb),
        in_specs=in_specs,
        out_specs=[out_spec, out_spec],
        out_shape=[out_sds, out_sds],
        compiler_params=_params("parallel", "arbitrary"),
        name=f"dilated_attn_d{dil}",
    )(qkv, qkv, qkv, qkv, qkv)


def _merge_groups_kernel(o0, o1, o2, l0, l1, l2, out_ref):
    for h in range(A_HEADS_PER_GROUP):
        a, b, c = l0[h], l1[h], l2[h]
        m = jnp.maximum(jnp.maximum(a, b), c)
        ea, eb, ec = jnp.exp(a - m), jnp.exp(b - m), jnp.exp(c - m)
        num = ea * o0[h] + eb * o1[h] + ec * o2[h]
        out_ref[:, h * HEAD_DIM:(h + 1) * HEAD_DIM] = (num / (ea + eb + ec)).astype(out_ref.dtype)


def _merge_groups(os_, lses):
    t = os_[0].shape[1]
    tm = min(t, 1024)
    spec = pl.BlockSpec((A_HEADS_PER_GROUP, tm, HEAD_DIM), lambda i: (0, i, 0))
    return pl.pallas_call(
        _merge_groups_kernel,
        grid=(t // tm,),
        in_specs=[spec] * 6,
        out_specs=pl.BlockSpec((tm, A_GROUP_WIDTH), lambda i: (i, 0)),
        out_shape=jax.ShapeDtypeStruct((t, A_GROUP_WIDTH), MXU_DTYPE),
        compiler_params=_params("parallel"),
        name="dilated_merge",
    )(*os_, *lses)


def _tree_row_sum(x, group):
    parts = [x[g * group:(g + 1) * group, :] for g in range(x.shape[0] // group)]
    while len(parts) > 1:
        parts = [a + b for a, b in zip(parts[0::2], parts[1::2])]
    return parts[0]


def _sparse_kernel(topk, idx_scale, qit_ref, wt_ref, ki_ref, qbt_ref, kb_ref, vt_ref,
                   out_ref, keys_ref, hi_ref, acc_ref, m_ref, s_idx0, s_idx1, s_att0, s_att1,
                   p0, p1, al0, al1, mt0, mt1):
    qb = pl.program_id(1)
    t0 = qb * QB_BLOCK
    n_pairs = (t0 + QB_BLOCK + 2 * KEY_TILE - 1) // (2 * KEY_TILE)
    n_tiles = 2 * n_pairs
    last_tile = n_tiles - 1
    lane_q = lax.broadcasted_iota(jnp.int32, (KEY_TILE, QB_BLOCK), 1) + t0
    row_id = lax.broadcasted_iota(jnp.int32, (KEY_TILE, QB_BLOCK), 0)
    query_pos = lax.broadcasted_iota(jnp.int32, (1, QB_BLOCK), 1) + t0
    few_keys = query_pos < topk
    wts = wt_ref[...] * idx_scale

    def tile_start(kt):
        return pl.multiple_of(kt * KEY_TILE, KEY_TILE)

    def run_pipelined(steps):
        def body(j, carry):
            steps(j * STEPS_PER_BODY, STEPS_PER_BODY)
            return carry

        n_bodies = n_tiles // STEPS_PER_BODY
        lax.fori_loop(0, n_bodies, body, 0)
        if STEPS_PER_BODY > 2:
            @pl.when(n_tiles % STEPS_PER_BODY != 0)
            def _():
                steps(n_bodies * STEPS_PER_BODY, 2)

    def idx_matmul(kt, buf):
        k_tile = ki_ref[pl.ds(tile_start(kt), KEY_TILE), 0:IDX_DIM]
        buf[...] = jnp.dot(k_tile, qit_ref[...], preferred_element_type=F32)

    def idx_reduce(kt, buf):
        r0 = tile_start(kt)
        acc = jnp.zeros((KEY_TILE, QB_BLOCK), F32)
        for h in range(IDX_HEADS):
            acc = acc + jnp.maximum(buf[:, h * QB_BLOCK:(h + 1) * QB_BLOCK], 0.0) * wts[h:h + 1, :]
        acc = jnp.where(jnp.abs(acc) < F32_MIN_NORMAL, 0.0, acc)
        acc = jnp.where(row_id + r0 <= lane_q, acc, -jnp.inf)
        bits = pltpu.bitcast(acc, jnp.int32)
        keys_ref[pl.ds(r0, KEY_TILE), :] = bits ^ ((bits >> 31) & 0x7FFFFFFF)
        hi_ref[pl.ds(r0, KEY_TILE), :] = pltpu.bitcast(bits & HI16_MASK, F32).astype(jnp.bfloat16)

    def score_steps(first, count):
        bufs = (s_idx0, s_idx1)
        for k in range(count):
            idx_matmul(jnp.minimum(first + k + 1, last_tile), bufs[(k + 1) % 2])
            idx_reduce(first + k, bufs[k % 2])

    idx_matmul(0, s_idx0)
    run_pipelined(score_steps)

    def sum_over_keys(partial, acc_rows, dtype):
        chunk_rows = COUNT_VREGS * acc_rows
        per_chunk = chunk_rows // KEY_TILE
        n_chunks = n_tiles // per_chunk

        def chunk(c, acc):
            return acc + partial(pl.multiple_of(c * chunk_rows, chunk_rows), chunk_rows)

        def tile(kt, acc):
            return acc + partial(tile_start(kt), KEY_TILE)

        acc = lax.fori_loop(0, n_chunks, chunk, jnp.zeros((acc_rows, QB_BLOCK), dtype))
        acc = lax.fori_loop(n_chunks * per_chunk, n_tiles, tile, acc)
        return jnp.sum(acc, axis=0, keepdims=True)

    def count_prefix_ge(prefix):
        prefix = jnp.where((prefix > 0) & (prefix < BF16_MIN_NORMAL), BF16_MIN_NORMAL, prefix)
        fbits = (prefix ^ ((prefix >> 15) & 0x7FFF)) << 16
        cand = pltpu.bitcast(fbits, F32).astype(jnp.bfloat16)

        def partial(r0, rows):
            hit = jnp.where(hi_ref[pl.ds(r0, rows), :] >= cand,
                            jnp.ones((rows, QB_BLOCK), jnp.bfloat16),
                            jnp.zeros((rows, QB_BLOCK), jnp.bfloat16))
            return _tree_row_sum(hit, BF16_ROWS).astype(F32)

        return sum_over_keys(partial, BF16_ROWS, F32)

    want = jnp.float32(topk)
    zero_prefix = jnp.zeros((1, QB_BLOCK), jnp.int32)
    cnt0 = count_prefix_ge(zero_prefix)
    take0 = cnt0 >= want
    prefix = jnp.where(take0, zero_prefix, -(2 ** 15))
    cnt = jnp.where(take0, cnt0, jnp.float32(2 ** 24))

    def prefix_step(i, carry):
        prefix, cnt = carry
        cand = prefix | lax.shift_left(jnp.int32(1), 14 - i)
        c = count_prefix_ge(cand)
        take = c >= want
        return jnp.where(take, cand, prefix), jnp.where(take, c, cnt)

    prefix, cnt = lax.fori_loop(0, 15, prefix_step, (prefix, cnt))

    def count_where(pred):
        def partial(r0, rows):
            rid = lax.broadcasted_iota(jnp.int32, (rows, QB_BLOCK), 0) + r0
            hit = jnp.where(pred(keys_ref[pl.ds(r0, rows), :], rid), 1, 0)
            return _tree_row_sum(hit, SUBLANES)
        return sum_over_keys(partial, SUBLANES, jnp.int32)

    def count_ge(cand):
        return count_where(lambda k, rid: k >= cand)

    def unresolved(cnt):
        return jnp.max(jnp.where(few_keys | (cnt == topk), 0, 1)) > 0

    def low_cond(carry):
        bit, _, cnt = carry
        return (bit >= 0) & unresolved(cnt)

    def low_step(carry):
        bit, tau, cnt = carry
        cand = tau | lax.shift_left(jnp.int32(1), bit)
        c = count_ge(cand)
        take = c >= topk
        return bit - 1, jnp.where(take, cand, tau), jnp.where(take, c, cnt)

    _, tau, n_ge = lax.while_loop(low_cond, low_step,
                                  (jnp.int32(15), prefix << 16, cnt.astype(jnp.int32)))
    tau = jnp.where(few_keys, KEY_NEG_INF + 1, tau)

    @pl.when(jnp.max(jnp.where(few_keys, 0, n_ge)) > topk)
    def _():
        need = topk - count_ge(tau + 1)

        def idx_step(i, last):
            cand = last | lax.shift_left(jnp.int32(1), 30 - i)
            below = count_where(lambda k, rid: (k == tau) & (rid < cand))
            return jnp.where(below < need, cand, last)

        last = lax.fori_loop(0, 31, idx_step, jnp.zeros((1, QB_BLOCK), jnp.int32))

        def demote(kt, carry):
            r0 = tile_start(kt)
            k = keys_ref[pl.ds(r0, KEY_TILE), :]
            drop = (k == tau) & (row_id + r0 > last)
            keys_ref[pl.ds(r0, KEY_TILE), :] = jnp.where(drop, tau - 1, k)
            return carry

        lax.fori_loop(0, n_tiles, demote, 0)

    def att_matmul(kt, s_buf, mt_buf):
        sel = keys_ref[pl.ds(tile_start(kt), KEY_TILE), :] >= tau
        bias = jnp.where(sel, 0.0, NEG_BIG)
        s = jnp.dot(kb_ref[pl.ds(tile_start(kt), KEY_TILE), :], qbt_ref[...],
                    preferred_element_type=F32)
        for h in range(B_HEADS):
            hs = slice(h * QB_BLOCK, (h + 1) * QB_BLOCK)
            sm = s[:, hs] + bias
            s_buf[:, hs] = sm
            mt_buf[:, hs] = jnp.max(sm, axis=0, keepdims=True)

    def softmax_tile(s_buf, mt_buf, p_buf, al_buf):
        m_old = m_ref[...]
        m_new = jnp.maximum(m_old, mt_buf[...])
        m_ref[...] = m_new
        al_buf[...] = jnp.exp2(m_old - m_new)
        for h in range(B_HEADS):
            hs = slice(h * QB_BLOCK, (h + 1) * QB_BLOCK)
            p_buf[:, hs] = jnp.exp2(s_buf[:, hs] - m_new[:, hs]).astype(p_buf.dtype)

    def value_matmul(kt, p_buf, al_buf):
        pv = jnp.dot(vt_ref[:, pl.ds(tile_start(kt), KEY_TILE)], p_buf[...],
                     preferred_element_type=F32)
        acc_ref[...] = acc_ref[...] * al_buf[...] + pv

    m_ref[...] = jnp.full(m_ref.shape, NEG_BIG, F32)
    acc_ref[...] = jnp.zeros(acc_ref.shape, F32)
    p1[...] = jnp.zeros(p1.shape, p1.dtype)
    al1[...] = jnp.ones(al1.shape, F32)
    att_matmul(0, s_att0, mt0)

    def attend_steps(first, count):
        s_bufs, mt_bufs, p_bufs, al_bufs = (s_att0, s_att1), (mt0, mt1), (p0, p1), (al0, al1)
        for k in range(count):
            cur, oth = k % 2, (k + 1) % 2
            att_matmul(jnp.minimum(first + k + 1, last_tile), s_bufs[oth], mt_bufs[oth])
            value_matmul(jnp.maximum(first + k - 1, 0), p_bufs[oth], al_bufs[oth])
            softmax_tile(s_bufs[cur], mt_bufs[cur], p_bufs[cur], al_bufs[cur])

    run_pipelined(attend_steps)
    value_matmul(last_tile, p1, al1)

    for h in range(B_HEADS):
        hs = slice(h * QB_BLOCK, (h + 1) * QB_BLOCK)
        o_t = acc_ref[0:HEAD_DIM, hs] / acc_ref[HEAD_DIM:HEAD_DIM + 1, hs]
        out_ref[:, h * HEAD_DIM:(h + 1) * HEAD_DIM] = o_t.T.astype(out_ref.dtype)


def _sparse_mixer(bproj, idx, wi_t, bsz, s):
    nqb = s // QB_BLOCK
    topk = min(B_TOPK_MAX, s // 4)
    idx_scale = (IDX_HEADS ** -0.5) * (IDX_DIM ** -0.5)
    nq = B_HEADS * QB_BLOCK
    qi = idx[:, :IDX_HEADS * IDX_DIM].reshape(bsz, nqb, QB_BLOCK, IDX_HEADS, IDX_DIM)
    qit = qi.transpose(0, 1, 4, 3, 2).reshape(bsz, nqb, IDX_DIM, IDX_HEADS * QB_BLOCK)
    qbq = bproj[:, :B_HEADS * HEAD_DIM].reshape(bsz, nqb, QB_BLOCK, B_HEADS, HEAD_DIM)
    qbt = qbq.transpose(0, 1, 4, 3, 2).reshape(bsz, nqb, HEAD_DIM, nq)
    vb = bproj[:, VB_COL:VB_COL + HEAD_DIM].reshape(bsz, s, HEAD_DIM)
    vt = jnp.concatenate([vb.transpose(0, 2, 1),
                          jnp.ones((bsz, 1, s), vb.dtype),
                          jnp.zeros((bsz, SUBLANES - 1, s), vb.dtype)], axis=1)
    v_rows = HEAD_DIM + SUBLANES
    b3 = bproj.reshape(bsz, s, B_COLS)
    idx3 = idx.reshape(bsz, s, IDX_COLS)
    kb_blk = KB_COL // LANES
    ki_blk = (IDX_HEADS * IDX_DIM) // LANES
    out = pl.pallas_call(
        functools.partial(_sparse_kernel, topk, idx_scale),
        grid=(bsz, nqb),
        in_specs=[
            pl.BlockSpec((None, None, IDX_DIM, IDX_HEADS * QB_BLOCK), lambda b, q: (b, q, 0, 0)),
            pl.BlockSpec((None, None, IDX_HEADS, QB_BLOCK), lambda b, q: (b, q, 0, 0)),
            pl.BlockSpec((None, s, LANES), lambda b, q: (b, 0, ki_blk)),
            pl.BlockSpec((None, None, HEAD_DIM, nq), lambda b, q: (b, q, 0, 0)),
            pl.BlockSpec((None, s, LANES), lambda b, q: (b, 0, kb_blk)),
            pl.BlockSpec((None, v_rows, s), lambda b, q: (b, 0, 0)),
        ],
        out_specs=pl.BlockSpec((None, QB_BLOCK, B_HEADS * HEAD_DIM), lambda b, q: (b, q, 0)),
        out_shape=jax.ShapeDtypeStruct((bsz, s, B_HEADS * HEAD_DIM), MXU_DTYPE),
        scratch_shapes=[pltpu.VMEM((s, QB_BLOCK), jnp.int32),
                        pltpu.VMEM((s, QB_BLOCK), jnp.bfloat16),
                        pltpu.VMEM((v_rows, nq), F32),
                        pltpu.VMEM((1, nq), F32),
                        pltpu.VMEM((KEY_TILE, IDX_HEADS * QB_BLOCK), F32),
                        pltpu.VMEM((KEY_TILE, IDX_HEADS * QB_BLOCK), F32),
                        pltpu.VMEM((KEY_TILE, nq), SOFTMAX_DTYPE),
                        pltpu.VMEM((KEY_TILE, nq), SOFTMAX_DTYPE),
                        pltpu.VMEM((KEY_TILE, nq), MXU_DTYPE),
                        pltpu.VMEM((KEY_TILE, nq), MXU_DTYPE),
                        pltpu.VMEM((1, nq), F32),
                        pltpu.VMEM((1, nq), F32),
                        pltpu.VMEM((1, nq), F32),
                        pltpu.VMEM((1, nq), F32)],
        compiler_params=_params("parallel", "arbitrary"),
        name="sparse_mixer",
    )(qit, wi_t, idx3, qbt, b3, vt)
    return out.reshape(bsz * s, B_HEADS * HEAD_DIM)


def _lru_kernel(ts, xc_ref, yc_ref, cw_ref, cb_ref, wa_ref, wx_ref, ba_ref, bx_ref, lam_ref,
                o_ref, xbuf, a_buf, b_buf, h_state):
    halo = SUBLANES

    @pl.when(pl.program_id(1) == 0)
    def _():
        xbuf[0:halo, :] = jnp.zeros((halo, xbuf.shape[1]), F32)
        h_state[...] = jnp.zeros(h_state.shape, F32)

    xbuf[halo:halo + ts, :] = xc_ref[...]
    xc = cb_ref[...] + cw_ref[0:1, :] * xbuf[halo - 3:halo - 3 + ts, :]
    for tap in range(1, LRU_CONV_W):
        lo = halo - (LRU_CONV_W - 1) + tap
        xc = xc + cw_ref[tap:tap + 1, :] * xbuf[lo:lo + ts, :]
    xbuf[0:halo, :] = xc_ref[ts - halo:ts, :]

    z = -lam_ref[...]
    softplus = jnp.maximum(z, 0.0) + jnp.log1p(jnp.exp(-jnp.abs(z)))
    bw = xc.shape[1] // LRU_BLOCKS
    for blk in range(LRU_BLOCKS):
        cs = slice(blk * bw, (blk + 1) * bw)
        xs = xc[:, cs]
        xs_m = xs.astype(MXU_DTYPE)
        r = _sigmoid(jnp.dot(xs_m, wa_ref[blk], preferred_element_type=F32) + ba_ref[:, cs])
        gate_i = _sigmoid(jnp.dot(xs_m, wx_ref[blk], preferred_element_type=F32) + bx_ref[:, cs])
        log_a = (-LRU_C) * r * softplus[:, cs]
        a = jnp.exp(log_a)
        a_buf[:, cs] = a
        b_buf[:, cs] = jnp.sqrt(-jnp.tanh(log_a) * (a * a + 1.0)) * gate_i * xs

    def step(t, h):
        h = a_buf[pl.ds(t, 1), :] * h + b_buf[pl.ds(t, 1), :]
        b_buf[pl.ds(t, 1), :] = h
        return h

    h_state[0:1, :] = lax.fori_loop(0, ts, step, h_state[0:1, :], unroll=8)
    o_ref[...] = (b_buf[...] * _gelu_tanh(yc_ref[...])).astype(o_ref.dtype)


def _lru_mixer(feat, bsz, s, width, conv_w, conv_b, wa, ba, wx, bx, lam):
    ts = min(s, 512)
    nt = s // ts
    row = lambda a: a.reshape(1, width).astype(F32)
    vec = pl.BlockSpec((1, width), lambda b, t: (0, 0))
    wspec = pl.BlockSpec(wa.shape, lambda b, t: (0, 0, 0))
    return pl.pallas_call(
        functools.partial(_lru_kernel, ts),
        grid=(bsz, nt),
        in_specs=[pl.BlockSpec((ts, width), lambda b, t: (b * nt + t, 0)),
                  pl.BlockSpec((ts, width), lambda b, t: (b * nt + t, 1)),
                  pl.BlockSpec((LRU_CONV_W, width), lambda b, t: (0, 0)),
                  vec, wspec, wspec, vec, vec, vec],
        out_specs=pl.BlockSpec((ts, width), lambda b, t: (b * nt + t, 0)),
        out_shape=jax.ShapeDtypeStruct((bsz * s, width), MXU_DTYPE),
        scratch_shapes=[pltpu.VMEM((ts + SUBLANES, width), F32),
                        pltpu.VMEM((ts, width), F32),
                        pltpu.VMEM((ts, width), F32),
                        pltpu.VMEM((SUBLANES, width), F32)],
        compiler_params=_params("parallel", "arbitrary"),
        name="rglru_mixer",
    )(feat, feat, conv_w.astype(F32), row(conv_b), wa.astype(MXU_DTYPE), wx.astype(MXU_DTYPE),
      row(ba), row(bx), row(lam))


def _gate_merge_kernel(oa, ob, oc, wa, wb, wc, g0, g1, g2, b0, b1, b2, out_ref):
    acc = _sigmoid(g0[...] + b0[...]) * jnp.dot(oa[...], wa[...], preferred_element_type=F32)
    acc = acc + _sigmoid(g1[...] + b1[...]) * jnp.dot(ob[...], wb[...], preferred_element_type=F32)
    acc = acc + _sigmoid(g2[...] + b2[...]) * jnp.dot(oc[...], wc[...], preferred_element_type=F32)
    out_ref[...] = acc.astype(out_ref.dtype)


def _gate_merge(out_a, out_b, out_c, wpa, wpb, wpc, feat, gate_col0, b_gate_row, d_model):
    t = out_a.shape[0]
    tm = min(t, 1024)
    tn = 512
    nj = d_model // tn
    g_blk = gate_col0 // tn

    def act(a):
        return pl.BlockSpec((tm, a.shape[1]), lambda i, j: (i, 0))

    def wgt(w):
        return pl.BlockSpec((w.shape[0], tn), lambda i, j: (0, j))

    gates = [pl.BlockSpec((tm, tn), functools.partial(lambda br, i, j: (i, g_blk + br * nj + j), br))
             for br in range(N_BRANCH)]
    biases = [pl.BlockSpec((1, tn), functools.partial(lambda br, i, j: (0, br * nj + j), br))
              for br in range(N_BRANCH)]
    return pl.pallas_call(
        _gate_merge_kernel,
        grid=(t // tm, nj),
        in_specs=[act(out_a), act(out_b), act(out_c), wgt(wpa), wgt(wpb), wgt(wpc)] + gates + biases,
        out_specs=pl.BlockSpec((tm, tn), lambda i, j: (i, j)),
        out_shape=jax.ShapeDtypeStruct((t, d_model), MXU_DTYPE),
        compiler_params=_params("parallel", "arbitrary"),
        name="gate_merge",
    )(out_a, out_b, out_c, wpa, wpb, wpc, feat, feat, feat, b_gate_row, b_gate_row, b_gate_row)


def _matmul_residual_kernel(a_ref, w_ref, r_ref, o_ref):
    o_ref[...] = r_ref[...] + jnp.dot(a_ref[...], w_ref[...], preferred_element_type=F32)


def _matmul_residual(a, w, resid, name):
    t, k = a.shape
    n = w.shape[1]
    tm = min(t, 1024)
    tn = 512
    return pl.pallas_call(
        _matmul_residual_kernel,
        grid=(t // tm, n // tn),
        in_specs=[pl.BlockSpec((tm, k), lambda i, j: (i, 0)),
                  pl.BlockSpec((k, tn), lambda i, j: (0, j)),
                  pl.BlockSpec((tm, tn), lambda i, j: (i, j))],
        out_specs=pl.BlockSpec((tm, tn), lambda i, j: (i, j)),
        out_shape=jax.ShapeDtypeStruct((t, n), F32),
        compiler_params=_params("parallel", "arbitrary"),
        name=name,
    )(a, w, resid)


def _ffn_up_kernel(tm, tiles_per_seq, x_ref, xh_ref, g_ref, wa_ref, wb_ref, cwa_ref, cwb_ref,
                   cba_ref, cbb_ref, o_ref, h_ref, hh_ref, ua_buf, ub_buf):
    halo = SUBLANES

    def normed(ref):
        xf = ref[...]
        ms = jnp.mean(xf * xf, axis=-1, keepdims=True)
        return (xf * lax.rsqrt(ms + NORM_EPS) * g_ref[...]).astype(MXU_DTYPE)

    @pl.when(pl.program_id(1) == 0)
    def _():
        h_ref[...] = normed(x_ref)
        hh_ref[...] = normed(xh_ref)

    keep = jnp.where(pl.program_id(0) % tiles_per_seq == 0, 0.0, 1.0)

    def conv_half(w_ref, cw_ref, cb_ref, buf):
        buf[0:halo, :] = keep * jnp.dot(hh_ref[...], w_ref[...], preferred_element_type=F32)
        buf[halo:halo + tm, :] = jnp.dot(h_ref[...], w_ref[...], preferred_element_type=F32)
        y = cb_ref[...] + cw_ref[0:1, :] * buf[halo - 2:halo - 2 + tm, :]
        for tap in range(1, FFN_CONV_W):
            lo = halo - (FFN_CONV_W - 1) + tap
            y = y + cw_ref[tap:tap + 1, :] * buf[lo:lo + tm, :]
        return y

    ya = conv_half(wa_ref, cwa_ref, cba_ref, ua_buf)
    yb = conv_half(wb_ref, cwb_ref, cbb_ref, ub_buf)
    o_ref[...] = (_gelu_tanh(ya) * yb).astype(o_ref.dtype)


def _ffn_up(x2d, s, gain, w_up, conv_w, conv_b_row):
    t, d = x2d.shape
    d_ff = w_up.shape[1] // 2
    tm = min(s, 1024)
    tn = 512
    nj = d_ff // tn
    hb = tm // SUBLANES
    return pl.pallas_call(
        functools.partial(_ffn_up_kernel, tm, s // tm),
        grid=(t // tm, nj),
        in_specs=[pl.BlockSpec((tm, d), lambda i, j: (i, 0)),
                  pl.BlockSpec((SUBLANES, d), lambda i, j: (jnp.maximum(i * hb - 1, 0), 0)),
                  pl.BlockSpec((1, d), lambda i, j: (0, 0)),
                  pl.BlockSpec((d, tn), lambda i, j: (0, j)),
                  pl.BlockSpec((d, tn), lambda i, j: (0, nj + j)),
                  pl.BlockSpec((FFN_CONV_W, tn), lambda i, j: (0, j)),
                  pl.BlockSpec((FFN_CONV_W, tn), lambda i, j: (0, nj + j)),
                  pl.BlockSpec((1, tn), lambda i, j: (0, j)),
                  pl.BlockSpec((1, tn), lambda i, j: (0, nj + j))],
        out_specs=pl.BlockSpec((tm, tn), lambda i, j: (i, j)),
        out_shape=jax.ShapeDtypeStruct((t, d_ff), MXU_DTYPE),
        scratch_shapes=[pltpu.VMEM((tm, d), MXU_DTYPE),
                        pltpu.VMEM((SUBLANES, d), MXU_DTYPE),
                        pltpu.VMEM((tm + SUBLANES, tn), F32),
                        pltpu.VMEM((tm + SUBLANES, tn), F32)],
        compiler_params=_params("parallel", "arbitrary"),
        name="ffn_up_conv_gate",
    )(x2d, x2d, gain, w_up, w_up, conv_w, conv_w, conv_b_row, conv_b_row)


def _pad_cols(parts, total, dtype):
    width = sum(p.shape[1] for p in parts)
    rows = parts[0].shape[0]
    if total > width:
        parts = list(parts) + [jnp.zeros((rows, total - width), parts[0].dtype)]
    return jnp.concatenate(parts, axis=1).astype(dtype)


def _temporal_block(x2d, bsz, s, tabs_head, tabs_idx, w_in, b_gate, qn_a, kn_a, qn_b, kn_b,
                    lru_conv_w, lru_conv_b, lru_wa, lru_ba, lru_wx, lru_bx, lru_lambda,
                    w_proj_a, w_proj_b, w_proj_c, w_out, norm_mix):
    d = x2d.shape[1]
    a_w = A_HEADS * HEAD_DIM
    b_w = B_HEADS * HEAD_DIM
    sizes = (a_w, a_w, a_w, b_w, HEAD_DIM, HEAD_DIM, IDX_HEADS * IDX_DIM, IDX_DIM, IDX_HEADS,
             d, d, N_BRANCH * d)
    offs = np.concatenate([[0], np.cumsum(sizes)])
    col = lambda k: w_in[:, offs[k]:offs[k + 1]]
    w_qa, w_ka, w_va, w_qb, w_kb, w_vb, w_qi, w_ki, w_wi, w_xc, w_yc, w_g = [col(k) for k in range(12)]

    gain_row = norm_mix.reshape(1, d).astype(F32)
    att_scale = HEAD_DIM ** -0.5
    ones = lambda k: jnp.ones((k,), F32)
    part_head = _partner_matrix(HEAD_DIM, ROT_DIM // 2)
    part_idx = _partner_matrix(IDX_DIM, IDX_ROT // 2)
    gw = A_GROUP_WIDTH

    a_gain = jnp.concatenate([jnp.tile(qn_a.astype(F32) * att_scale, A_HEADS_PER_GROUP),
                              jnp.tile(kn_a.astype(F32), A_HEADS_PER_GROUP), ones(gw)]).reshape(1, 3 * gw)
    a_kinds = [NORM_ROPE] * 4 + [PLAIN] * 2
    os_, lses = [], []
    for g, (window, dil) in enumerate(DIL_GROUPS):
        gs = slice(g * gw, (g + 1) * gw)
        w_g3 = jnp.concatenate([w_qa[:, gs], w_ka[:, gs], w_va[:, gs]], axis=1).astype(MXU_DTYPE)
        qkv = _norm_proj_heads(x2d, gain_row, w_g3, part_head, tabs_head, a_gain, a_kinds,
                               dil=dil, name=f"proj_a_d{dil}")
        o, lse = _dilated_group(qkv, bsz, s, window, dil)
        os_.append(o)
        lses.append(lse)
    out_a = _merge_groups(os_, lses)

    b_gain = jnp.concatenate([jnp.tile(qn_b.astype(F32) * (att_scale * LOG2_E), B_HEADS),
                              kn_b.astype(F32), ones(B_COLS - b_w - HEAD_DIM)]).reshape(1, B_COLS)
    zeros = jnp.zeros((d, HEAD_DIM), w_in.dtype)
    w_b = jnp.concatenate([w_qb, w_kb, zeros, w_vb, zeros], axis=1).astype(MXU_DTYPE)
    b_kinds = [NORM_ROPE] * (b_w // EP_CHUNK + 1) + [PLAIN]
    bproj = _norm_proj_heads(x2d, gain_row, w_b, part_head, tabs_head, b_gain, b_kinds,
                             dil=1, name="proj_b").reshape(bsz * s, B_COLS)
    idx = _norm_proj_heads(x2d, gain_row, _pad_cols([w_qi, w_ki], IDX_COLS, MXU_DTYPE), part_idx,
                           tabs_idx, jnp.ones((1, IDX_COLS), F32), [ROPE] * (IDX_COLS // EP_CHUNK),
                           dil=1, name="proj_idx").reshape(bsz * s, IDX_COLS)
    gate_col0 = 2 * d
    wi_col0 = gate_col0 + N_BRANCH * d
    feat_cols = -(-(wi_col0 + IDX_HEADS) // FEAT_TILE) * FEAT_TILE
    feat = _norm_proj(x2d, gain_row, _pad_cols([w_xc, w_yc, w_g, w_wi], feat_cols, MXU_DTYPE),
                      tn=FEAT_TILE, out_dtype=F32, name="proj_feat")
    wi = feat[:, wi_col0:wi_col0 + IDX_HEADS].reshape(bsz, s // QB_BLOCK, QB_BLOCK, IDX_HEADS)
    out_b = _sparse_mixer(bproj, idx, wi.transpose(0, 1, 3, 2), bsz, s)

    out_c = _lru_mixer(feat, bsz, s, d, lru_conv_w, lru_conv_b, lru_wa, lru_ba, lru_wx, lru_bx,
                       lru_lambda)

    merged = _gate_merge(out_a, out_b, out_c, w_proj_a.astype(MXU_DTYPE),
                         w_proj_b.astype(MXU_DTYPE), w_proj_c.astype(MXU_DTYPE), feat, gate_col0,
                         b_gate.reshape(1, N_BRANCH * d).astype(F32), d)
    return _matmul_residual(merged, w_out.astype(MXU_DTYPE), x2d, "out_proj_residual")


def _conv_ffn_block(x2d, s, norm_ffn, ffn_w_up, ffn_conv_w, ffn_conv_b, ffn_w_down):
    d = x2d.shape[1]
    act = _ffn_up(x2d, s, norm_ffn.reshape(1, d).astype(F32), ffn_w_up.astype(MXU_DTYPE),
                  ffn_conv_w.astype(F32), ffn_conv_b.reshape(1, -1).astype(F32))
    return _matmul_residual(act, ffn_w_down.astype(MXU_DTYPE), x2d, "ffn_down_residual")


def kernel(x, positions, w_in, b_gate, qn_a, kn_a, qn_b, kn_b, lru_conv_w, lru_conv_b, lru_wa,
           lru_ba, lru_wx, lru_bx, lru_lambda, w_proj_a, w_proj_b, w_proj_c, w_out, norm_mix,
           norm_ffn, ffn_w_up, ffn_conv_w, ffn_conv_b, ffn_w_down):
    bsz, s, d = x.shape
    assert s % (DIL_GROUPS[-1][1] * A_BLOCK) == 0 and s % (2 * KEY_TILE) == 0
    x2d = x.reshape(bsz * s, d)
    pos_col = positions.astype(F32).reshape(bsz * s, 1)
    tabs_head = _rope_tables(pos_col, ROT_DIM, HEAD_DIM)
    tabs_idx = _rope_tables(pos_col, IDX_ROT, IDX_DIM)
    for layer in range(w_in.shape[0]):
        x2d = _temporal_block(
            x2d, bsz, s, tabs_head, tabs_idx, w_in[layer], b_gate[layer], qn_a[layer],
            kn_a[layer], qn_b[layer], kn_b[layer], lru_conv_w[layer], lru_conv_b[layer],
            lru_wa[layer], lru_ba[layer], lru_wx[layer], lru_bx[layer], lru_lambda[layer],
            w_proj_a[layer], w_proj_b[layer], w_proj_c[layer], w_out[layer], norm_mix[layer])
        x2d = _conv_ffn_block(x2d, s, norm_ffn[layer], ffn_w_up[layer], ffn_conv_w[layer],
                              ffn_conv_b[layer], ffn_w_down[layer])
    return x2d.reshape(bsz, s, d)
```
